```python
import math
import jax, jax.numpy as jnp
from jax import lax
import numpy as np


D_MODEL = 2048
BATCH = 2
SEQ = 16384
DEPTH = 1

SB_HEADS = 8
SB_HEAD_DIM = 128
SB_WIDTH = SB_HEADS * SB_HEAD_DIM
HG_HEADS = 8
HG_HEAD_DIM = 128
HG_WIDTH = HG_HEADS * HG_HEAD_DIM
D_FF = 4 * D_MODEL
PLE_DIM = 256
Q_BLOCK = 128
HG_CHUNK = 64
EPS = 1e-6

IN_WIDTH = 3 * SB_WIDTH + 4 * HG_WIDTH + 2 * D_MODEL
SPLIT_POINTS = (SB_WIDTH, 2 * SB_WIDTH, 3 * SB_WIDTH, 3 * SB_WIDTH + HG_WIDTH, 3 * SB_WIDTH + 2 * HG_WIDTH, 3 * SB_WIDTH + 3 * HG_WIDTH, 3 * SB_WIDTH + 4 * HG_WIDTH, 3 * SB_WIDTH + 4 * HG_WIDTH + D_MODEL)

kernel_name = 'hybrid_stickbreaking_hgrn2_gated_block'


def rmsnorm(x, g):
    xf = x.astype(jnp.float32)
    var = jnp.mean(xf * xf, axis=-1, keepdims=True)
    return (xf * lax.rsqrt(var + EPS) * g.astype(jnp.float32)).astype(x.dtype)


def split_heads(t, n_heads):
    b, s, w = t.shape
    return t.reshape(b, s, n_heads, w // n_heads).transpose(0, 2, 1, 3)


def merge_heads(t):
    b, h, s, d = t.shape
    return t.transpose(0, 2, 1, 3).reshape(b, s, h * d)


def stick_breaking_attention(q, k, v):
    b, h, s, dh = q.shape
    n_blk = s // Q_BLOCK
    scale = dh ** -0.5
    kf = k.astype(jnp.float32)
    vf = v.astype(jnp.float32)
    key_pos = jnp.arange(s)

    def block(i):
        start = i * Q_BLOCK
        qb = lax.dynamic_slice_in_dim(q, start, Q_BLOCK, axis=2).astype(jnp.float32)
        z = jnp.einsum('bhtd,bhsd->bhts', qb, kf) * scale
        qpos = start + jnp.arange(Q_BLOCK)
        mask = key_pos[None, :] < qpos[:, None]
        log_keep = jnp.where(mask, jax.nn.log_sigmoid(-z), 0.0)
        later = lax.cumsum(log_keep, axis=3, reverse=True) - log_keep
        w = jnp.where(mask, jnp.exp(jax.nn.log_sigmoid(z) + later), 0.0)
        return jnp.einsum('bhts,bhsd->bhtd', w, vf)

    out = lax.map(block, jnp.arange(n_blk))
    out = jnp.transpose(out, (1, 2, 0, 3, 4)).reshape(b, h, s, dh)
    return out.astype(q.dtype)


def hgrn2_chunkwise(q, log_f, k, v):
    b, h, s, dk = q.shape
    dv = v.shape[-1]
    n = s // HG_CHUNK

    def to_chunks(t):
        return jnp.moveaxis(t.reshape(b, h, n, HG_CHUNK, t.shape[-1]), 2, 0)

    qc, gc, kc, vc = to_chunks(q), to_chunks(log_f), to_chunks(k), to_chunks(v)
    causal = jnp.tril(jnp.ones((HG_CHUNK, HG_CHUNK), dtype=bool))

    def step(state, inp):
        qi, gi, ki, vi = inp
        bcum = jnp.cumsum(gi, axis=2)
        diff = bcum[:, :, :, None, :] - bcum[:, :, None, :, :]
        decay = jnp.exp(jnp.where(causal[:, :, None], diff, -jnp.inf))
        scores = jnp.einsum('bhtk,bhtsk,bhsk->bhts', qi, decay, ki)
        o = jnp.einsum('bhts,bhsv->bhtv', scores, vi)
        o = o + jnp.einsum('bhtk,bhkv->bhtv', qi * jnp.exp(bcum), state)
        b_last = bcum[:, :, -1:, :]
        k_dec = ki * jnp.exp(b_last - bcum)
        state = jnp.exp(b_last[:, :, 0, :])[..., None] * state + jnp.einsum('bhsk,bhsv->bhkv', k_dec, vi)
        return state, o

    s0 = jnp.zeros((b, h, dk, dv), jnp.float32)
    _, oc = lax.scan(step, s0, (qc, gc, kc, vc))
    return jnp.moveaxis(oc, 0, 2).reshape(b, h, s, dv)


def hgrn_lower_bounds(lb_logits):
    pr = jax.nn.softmax(lb_logits.astype(jnp.float32), axis=0)
    return jnp.cumsum(pr, axis=0)[:DEPTH]


def headwise_rmsnorm(o, g, n_heads):
    b, s, w = o.shape
    of = o.reshape(b, s, n_heads, w // n_heads).astype(jnp.float32)
    var = jnp.mean(of * of, axis=-1, keepdims=True)
    of = of * lax.rsqrt(var + EPS)
    return of.reshape(b, s, w) * g.astype(jnp.float32)


def setup_inputs(seed: int = 0) -> dict:
    key = jax.random.key(seed)
    ks = jax.random.split(key, 16)
    f32 = jnp.float32

    def nrm(k, shape, scale):
        return jax.random.normal(k, shape, f32) * scale

    return {
        'x': nrm(ks[0], (BATCH, SEQ, D_MODEL), 1.0),
        'p': nrm(ks[1], (DEPTH, BATCH, SEQ, PLE_DIM), 1.0),
        'mix_norm_g': 1.0 + nrm(ks[2], (DEPTH, D_MODEL), 0.1),
        'w_in': nrm(ks[3], (DEPTH, D_MODEL, IN_WIDTH), D_MODEL ** -0.5),
        'hgrn_lb_logits': nrm(ks[4], (DEPTH + 1, HG_WIDTH), 0.5),
        'hgrn_out_norm_g': 1.0 + nrm(ks[5], (DEPTH, HG_WIDTH), 0.1),
        'w_o_sb': nrm(ks[6], (DEPTH, SB_WIDTH, D_MODEL), SB_WIDTH ** -0.5),
        'w_o_hg': nrm(ks[7], (DEPTH, HG_WIDTH, D_MODEL), HG_WIDTH ** -0.5),
        'w_out': nrm(ks[8], (DEPTH, D_MODEL, D_MODEL), D_MODEL ** -0.5),
        'mlp_norm_g': 1.0 + nrm(ks[9], (DEPTH, D_MODEL), 0.1),
        'w_up': nrm(ks[10], (DEPTH, D_MODEL, D_FF), D_MODEL ** -0.5),
        'w_down': nrm(ks[11], (DEPTH, D_FF, D_MODEL), D_FF ** -0.5),
        'w_ple_proj': nrm(ks[12], (DEPTH, PLE_DIM, D_MODEL), PLE_DIM ** -0.5),
        'w_ple_gate': nrm(ks[13], (DEPTH, D_MODEL, D_MODEL), D_MODEL ** -0.5),
        'final_norm_g': 1.0 + nrm(ks[14], (D_MODEL,), 0.1),
    }


def reference(x, p, mix_norm_g, w_in, hgrn_lb_logits, hgrn_out_norm_g, w_o_sb, w_o_hg, w_out, mlp_norm_g, w_up, w_down, w_ple_proj, w_ple_gate, final_norm_g):
    f32 = jnp.float32
    lower_bounds = hgrn_lower_bounds(hgrn_lb_logits)
    for i in range(DEPTH):
        h = rmsnorm(x, mix_norm_g[i])
        proj = h @ w_in[i]
        q_sb, k_sb, v_sb, f_raw, i_hg, q_hg, g_hg, gate_sb, gate_hg = jnp.split(proj, SPLIT_POINTS, axis=-1)

        y_sb = merge_heads(stick_breaking_attention(split_heads(q_sb, SB_HEADS), split_heads(k_sb, SB_HEADS), split_heads(v_sb, SB_HEADS)))

        lb = lower_bounds[i]
        f = lb + (1.0 - lb) * jax.nn.sigmoid(f_raw.astype(f32))
        log_f = jnp.log(f)
        k_hg = 1.0 - f
        q_act = jax.nn.silu(q_hg.astype(f32))
        o_hg = hgrn2_chunkwise(split_heads(q_act, HG_HEADS), split_heads(log_f, HG_HEADS), split_heads(k_hg, HG_HEADS), split_heads(i_hg.astype(f32), HG_HEADS))
        o_hg = merge_heads(o_hg)
        y_hg = (headwise_rmsnorm(o_hg, hgrn_out_norm_g[i], HG_HEADS) * jax.nn.silu(g_hg.astype(f32))).astype(x.dtype)

        y = jax.nn.sigmoid(gate_sb) * (y_sb @ w_o_sb[i]) + jax.nn.sigmoid(gate_hg) * (y_hg @ w_o_hg[i])
        x = x + y @ w_out[i]

        h = rmsnorm(x, mlp_norm_g[i])
        x = x + jnp.square(jax.nn.relu(h @ w_up[i])) @ w_down[i]

        x = x + jax.nn.sigmoid(x @ w_ple_gate[i]) * (p[i] @ w_ple_proj[i])
    return rmsnorm(x, final_norm_g)
```

```python
import functools

import jax
import jax.numpy as jnp
from jax import lax
from jax.experimental import pallas as pl
from jax.experimental.pallas import tpu as pltpu

EPS = 1e-6
HEAD_DIM = 128
SB_HEADS = 8
HG_HEADS = 8
F32 = jnp.float32
BF16 = jnp.bfloat16

VMEM_LIMIT_BYTES = 56 * 1024 * 1024

EXP_ZERO_BOUND = -104.0


def _params(sem):
    return pltpu.CompilerParams(dimension_semantics=sem, vmem_limit_bytes=VMEM_LIMIT_BYTES)


def _sigmoid(x):
    return 1.0 / (1.0 + jnp.exp(-x))


def _rms_scale(x):
    return lax.rsqrt(jnp.mean(x * x, axis=-1, keepdims=True) + EPS)


def _dot(a, b):
    return jnp.dot(a, b, preferred_element_type=F32)


def _dot_nt(a, b):
    return lax.dot_general(a, b, (((1,), (1,)), ((), ())), preferred_element_type=F32)


def _dot_tn(a, b):
    return lax.dot_general(a, b, (((0,), (0,)), ((), ())), preferred_element_type=F32)


def _in_proj_kernel(x_ref, g_ref, w_ref, qkv_ref, fraw_ref, iqg_ref, gates_ref, h_scr, *, q_scale):
    j = pl.program_id(1)

    @pl.when(j == 0)
    def _():
        x = x_ref[...]
        h_scr[...] = (x * _rms_scale(x) * g_ref[...]).astype(BF16)

    def proj():
        return _dot(h_scr[...], w_ref[...])

    @pl.when(j == 0)
    def _():
        qkv_ref[...] = (proj() * q_scale).astype(BF16)

    @pl.when((j >= 1) & (j < 3))
    def _():
        qkv_ref[...] = proj().astype(BF16)

    @pl.when(j == 3)
    def _():
        fraw_ref[...] = proj()

    @pl.when((j >= 4) & (j < 7))
    def _():
        iqg_ref[...] = proj().astype(BF16)

    @pl.when(j >= 7)
    def _():
        gates_ref[...] = _sigmoid(proj()).astype(BF16)


def _in_proj(x2, g, w_bf16, *, tm):
    n, d = x2.shape
    width = w_bf16.shape[1]
    tn = 1024
    assert width == 11 * tn and n % tm == 0
    nj = width // tn
    kern = functools.partial(_in_proj_kernel, q_scale=HEAD_DIM ** -0.5)
    return pl.pallas_call(
        kern,
        grid=(n // tm, nj),
        in_specs=[
            pl.BlockSpec((tm, d), lambda i, j: (i, 0)),
            pl.BlockSpec((1, d), lambda i, j: (0, 0)),
            pl.BlockSpec((d, tn), lambda i, j: (0, j)),
        ],
        out_specs=[
            pl.BlockSpec((tm, tn), lambda i, j: (i, jnp.clip(j, 0, 2))),
            pl.BlockSpec((tm, tn), lambda i, j: (i, 0)),
            pl.BlockSpec((tm, tn), lambda i, j: (i, jnp.clip(j - 4, 0, 2))),
            pl.BlockSpec((tm, tn), lambda i, j: (i, jnp.clip(j - 7, 0, 3))),
        ],
        out_shape=[
            jax.ShapeDtypeStruct((n, 3 * tn), BF16),
            jax.ShapeDtypeStruct((n, tn), F32),
            jax.ShapeDtypeStruct((n, 3 * tn), BF16),
            jax.ShapeDtypeStruct((n, 4 * tn), BF16),
        ],
        scratch_shapes=[pltpu.VMEM((tm, d), BF16)],
        compiler_params=_params(("parallel", "arbitrary")),
        name="in_proj",
    )(x2, g, w_bf16)


def _sb_attn_kernel(q_ref, k_ref, v_ref, o_ref, acc_scr, r_scr, *, tq, tk, heads):
    i = pl.program_id(2)
    q_start = i * tq
    acc_scr[...] = jnp.zeros_like(acc_scr)
    r_scr[...] = jnp.zeros_like(r_scr)

    jj = lax.broadcasted_iota(jnp.int32, (tk, 2 * tk), 0)
    ss = lax.broadcasted_iota(jnp.int32, (tk, 2 * tk), 1)
    later_mat = jnp.where((jj > ss) | (ss >= tk), 1.0, 0.0).astype(BF16)

    row = q_start + lax.broadcasted_iota(jnp.int32, (tq, tk), 0)
    col0 = lax.broadcasted_iota(jnp.int32, (tq, tk), 1)

    def body(carry):
        j, _ = carry
        k_start = pl.multiple_of(j * tk, tk)
        mask = (k_start + col0) < row
        r_max = None
        for h in range(heads):
            lanes = slice(h * HEAD_DIM, (h + 1) * HEAD_DIM)
            q = q_ref[0, :, lanes]
            k = k_ref[0, pl.ds(k_start, tk), lanes]
            v = v_ref[0, pl.ds(k_start, tk), lanes]
            z = _dot_nt(q, k)
            softplus = jnp.maximum(z, 0.0) + jnp.log1p(jnp.exp(-jnp.abs(z)))
            log_keep = jnp.where(mask, -softplus, 0.0)
            hi = log_keep.astype(BF16)
            lo = (log_keep - hi.astype(F32)).astype(BF16)
            cs = _dot(hi, later_mat) + _dot(lo, later_mat)
            r_old = r_scr[h]
            later = cs[:, :tk] + r_old
            w = jnp.where(mask, jnp.exp(z + log_keep + later), 0.0)
            acc_scr[h] += _dot(w.astype(BF16), v)
            r_new = r_old + cs[:, tk:]
            r_scr[h] = r_new
            m = jnp.max(r_new)
            r_max = m if r_max is None else jnp.maximum(r_max, m)
        return j - 1, r_max

    def cond(carry):
        j, r_max = carry
        return (j >= 0) & (r_max > EXP_ZERO_BOUND)

    j0 = (q_start + tq) // tk - 1
    lax.while_loop(cond, body, (j0, jnp.float32(0.0)))

    for h in range(heads):
        o_ref[0, :, h * HEAD_DIM:(h + 1) * HEAD_DIM] = acc_scr[h].astype(o_ref.dtype)


def _sb_attn(qkv, *, tq):
    b, s, w3 = qkv.shape
    width = w3 // 3
    heads = 2
    gw = heads * HEAD_DIM
    ng = width // gw
    tk = HEAD_DIM
    assert s % tq == 0 and tq % tk == 0
    kern = functools.partial(_sb_attn_kernel, tq=tq, tk=tk, heads=heads)
    return pl.pallas_call(
        kern,
        grid=(b, ng, s // tq),
        in_specs=[
            pl.BlockSpec((1, tq, gw), lambda bi, g, i: (bi, i, g)),
            pl.BlockSpec((1, s, gw), lambda bi, g, i: (bi, 0, ng + g)),
            pl.BlockSpec((1, s, gw), lambda bi, g, i: (bi, 0, 2 * ng + g)),
        ],
        out_specs=pl.BlockSpec((1, tq, gw), lambda bi, g, i: (bi, i, g)),
        out_shape=jax.ShapeDtypeStruct((b, s, width), BF16),
        scratch_shapes=[
            pltpu.VMEM((heads, tq, HEAD_DIM), F32),
            pltpu.VMEM((heads, tq, tk), F32),
        ],
        compiler_params=_params(("parallel", "parallel", "arbitrary")),
        name="sb_attn",
    )(qkv, qkv, qkv)


HG_CHUNK = 128
HG_SUB = 16


def _hgrn_kernel(fraw_ref, i_ref, q_ref, g_ref, lbl_ref, gn_ref, o_ref,
                 st_scr, b_scr, k_scr, q_scr, v_scr, *, layer, chunks):
    C, c = HG_CHUNK, HG_SUB

    @pl.when(pl.program_id(2) == 0)
    def _():
        st_scr[...] = jnp.zeros_like(st_scr)

    logits = lbl_ref[...]
    e = jnp.exp(logits - jnp.max(logits, axis=0, keepdims=True))
    lb = jnp.sum(e[:layer + 1], axis=0, keepdims=True) / jnp.sum(e, axis=0, keepdims=True)
    gn = gn_ref[...]

    tt = lax.broadcasted_iota(jnp.int32, (C, C), 0)
    uu = lax.broadcasted_iota(jnp.int32, (C, C), 1)
    tri = jnp.where(uu <= tt, 1.0, 0.0).astype(BF16)
    ones = jnp.ones((HEAD_DIM, HEAD_DIM), BF16)
    sub_row = lax.broadcasted_iota(jnp.int32, (c, HEAD_DIM), 0)

    def chunk(ci, carry):
        r0 = pl.multiple_of(ci * C, C)
        rows = pl.ds(r0, C)
        f = lb + (1.0 - lb) * _sigmoid(fraw_ref[0, rows, :])
        logf = jnp.log(f)
        p0 = logf.astype(BF16)
        r1 = logf - p0.astype(F32)
        p1 = r1.astype(BF16)
        p2 = (r1 - p1.astype(F32)).astype(BF16)
        b = _dot(tri, p0) + _dot(tri, p1) + _dot(tri, p2)
        qf = q_ref[0, rows, :].astype(F32)
        qa = qf * _sigmoid(qf)
        b_scr[...] = b
        k_scr[...] = 1.0 - f
        q_scr[...] = qa
        v_scr[...] = i_ref[0, rows, :].astype(F32)

        st = st_scr[...]
        o_inter = _dot_nt((qa * jnp.exp(b)).astype(BF16), st.astype(BF16))

        pieces = []
        for I in range(C // c):
            lo = I * c
            bI = b_scr[lo:lo + c, :]
            qI = q_scr[lo:lo + c, :]
            oI = o_inter[lo:lo + c, :]
            if I > 0:
                m = b_scr[lo - 1:lo, :]
                qs = (qI * jnp.exp(bI - m)).astype(BF16)
                ks = (k_scr[0:lo, :] * jnp.exp(m - b_scr[0:lo, :])).astype(BF16)
                sc = _dot_nt(qs, ks)
                oI = oI + _dot(sc.astype(BF16), v_scr[0:lo, :].astype(BF16))
            xs = []
            for s in range(c):
                d = bI - b_scr[lo + s:lo + s + 1, :]
                x = qI * jnp.exp(jnp.minimum(d, 0.0)) * k_scr[lo + s:lo + s + 1, :]
                xs.append(jnp.where(sub_row >= s, x, 0.0).astype(BF16))
            a = _dot(jnp.concatenate(xs, axis=0), ones)
            for s in range(c):
                oI = oI + a[s * c:(s + 1) * c, :] * v_scr[lo + s:lo + s + 1, :]
            pieces.append(oI)
        o = jnp.concatenate(pieces, axis=0)

        b_last = b_scr[C - 1:C, :]
        kd = (k_scr[...] * jnp.exp(b_last - b)).astype(BF16)
        st_scr[...] = st * jnp.exp(b_last) + _dot_tn(v_scr[...].astype(BF16), kd)

        gf = g_ref[0, rows, :].astype(F32)
        y = o * _rms_scale(o) * gn * (gf * _sigmoid(gf))
        o_ref[0, rows, :] = y.astype(o_ref.dtype)
        return carry

    lax.fori_loop(0, chunks, chunk, 0)


def _hgrn(fraw, iqg, lb_logits, gnorm, *, layer, tt):
    b, s, width = fraw.shape
    nh = width // HEAD_DIM
    assert s % tt == 0 and tt % HG_CHUNK == 0
    nl = lb_logits.shape[0]
    kern = functools.partial(_hgrn_kernel, layer=layer, chunks=tt // HG_CHUNK)
    blk = lambda off: pl.BlockSpec((1, tt, HEAD_DIM), lambda bi, h, t: (bi, t, off + h))
    return pl.pallas_call(
        kern,
        grid=(b, nh, s // tt),
        in_specs=[
            blk(0),
            blk(0),
            blk(nh),
            blk(2 * nh),
            pl.BlockSpec((nl, HEAD_DIM), lambda bi, h, t: (0, h)),
            pl.BlockSpec((1, HEAD_DIM), lambda bi, h, t: (0, h)),
        ],
        out_specs=blk(0),
        out_shape=jax.ShapeDtypeStruct((b, s, width), BF16),
        scratch_shapes=[
            pltpu.VMEM((HEAD_DIM, HEAD_DIM), F32),
            pltpu.VMEM((HG_CHUNK, HEAD_DIM), F32),
            pltpu.VMEM((HG_CHUNK, HEAD_DIM), F32),
            pltpu.VMEM((HG_CHUNK, HEAD_DIM), F32),
            pltpu.VMEM((HG_CHUNK, HEAD_DIM), F32),
        ],
        compiler_params=_params(("parallel", "parallel", "arbitrary")),
        name="hgrn",
    )(fraw, iqg, iqg, iqg, lb_logits, gnorm)


def _merge_kernel(x_ref, ysb_ref, yhg_ref, gsb_ref, ghg_ref, wsb_ref, whg_ref, wout_ref, o_ref):
    a = _dot(ysb_ref[...], wsb_ref[...])
    b = _dot(yhg_ref[...], whg_ref[...])
    y = gsb_ref[...].astype(F32) * a + ghg_ref[...].astype(F32) * b
    o_ref[...] = x_ref[...] + _dot(y.astype(BF16), wout_ref[...])


def _const_spec(shape):
    return pl.BlockSpec(shape, lambda i: (0,) * len(shape), pipeline_mode=pl.Buffered(1))


def _merge(x2, ysb, yhg, gates, wsb, whg, wout, *, tm):
    n, d = x2.shape
    wb = ysb.shape[1]
    return pl.pallas_call(
        _merge_kernel,
        grid=(n // tm,),
        in_specs=[
            pl.BlockSpec((tm, d), lambda i: (i, 0)),
            pl.BlockSpec((tm, wb), lambda i: (i, 0)),
            pl.BlockSpec((tm, wb), lambda i: (i, 0)),
            pl.BlockSpec((tm, d), lambda i: (i, 0)),
            pl.BlockSpec((tm, d), lambda i: (i, 1)),
            _const_spec(wsb.shape),
            _const_spec(whg.shape),
            _const_spec(wout.shape),
        ],
        out_specs=pl.BlockSpec((tm, d), lambda i: (i, 0)),
        out_shape=jax.ShapeDtypeStruct((n, d), F32),
        compiler_params=_params(("parallel",)),
        name="merge",
    )(x2, ysb, yhg, gates, gates, wsb, whg, wout)


def _mlp_kernel(x_ref, g_ref, wup_ref, wdn_ref, o_ref, h_scr, acc_scr):
    f = pl.program_id(1)

    @pl.when(f == 0)
    def _():
        x = x_ref[...]
        h_scr[...] = (x * _rms_scale(x) * g_ref[...]).astype(BF16)
        acc_scr[...] = x

    u = jnp.maximum(_dot(h_scr[...], wup_ref[...]), 0.0)
    acc_scr[...] += _dot((u * u).astype(BF16), wdn_ref[...])

    @pl.when(f == pl.num_programs(1) - 1)
    def _():
        o_ref[...] = acc_scr[...]


def _mlp(x2, g, wup, wdn, *, tm, tf):
    n, d = x2.shape
    dff = wup.shape[1]
    return pl.pallas_call(
        _mlp_kernel,
        grid=(n // tm, dff // tf),
        in_specs=[
            pl.BlockSpec((tm, d), lambda i, f: (i, 0)),
            pl.BlockSpec((1, d), lambda i, f: (0, 0)),
            pl.BlockSpec((d, tf), lambda i, f: (0, f)),
            pl.BlockSpec((tf, d), lambda i, f: (f, 0)),
        ],
        out_specs=pl.BlockSpec((tm, d), lambda i, f: (i, 0)),
        out_shape=jax.ShapeDtypeStruct((n, d), F32),
        scratch_shapes=[pltpu.VMEM((tm, d), BF16), pltpu.VMEM((tm, d), F32)],
        compiler_params=_params(("parallel", "arbitrary")),
        name="mlp",
    )(x2, g, wup, wdn)


def _ple_kernel(x_ref, p_ref, wg_ref, wp_ref, fg_ref, o_ref, *, final_norm):
    x = x_ref[...]
    gate = _sigmoid(_dot(x.astype(BF16), wg_ref[...]))
    y = x + gate * _dot(p_ref[...].astype(BF16), wp_ref[...])
    if final_norm:
        y = y * _rms_scale(y) * fg_ref[...]
    o_ref[...] = y


def _ple(x2, p2, wg, wp, fg, *, tm, final_norm):
    n, d = x2.shape
    pd = p2.shape[1]
    kern = functools.partial(_ple_kernel, final_norm=final_norm)
    return pl.pallas_call(
        kern,
        grid=(n // tm,),
        in_specs=[
            pl.BlockSpec((tm, d), lambda i: (i, 0)),
            pl.BlockSpec((tm, pd), lambda i: (i, 0)),
            _const_spec(wg.shape),
            _const_spec(wp.shape),
            _const_spec(fg.shape),
        ],
        out_specs=pl.BlockSpec((tm, d), lambda i: (i, 0)),
        out_shape=jax.ShapeDtypeStruct((n, d), F32),
        compiler_params=_params(("parallel",)),
        name="ple",
    )(x2, p2, wg, wp, fg)


def _tile(n, pref):
    t = min(n, pref)
    assert n % t == 0
    return t


def kernel(x, p, mix_norm_g, w_in, hgrn_lb_logits, hgrn_out_norm_g, w_o_sb, w_o_hg, w_out,
           mlp_norm_g, w_up, w_down, w_ple_proj, w_ple_gate, final_norm_g):
    bsz, seq, d = x.shape
    depth = w_in.shape[0]
    n = bsz * seq
    sb_width = SB_HEADS * HEAD_DIM
    hg_width = HG_HEADS * HEAD_DIM
    x2 = x.reshape(n, d)
    for i in range(depth):
        qkv, fraw, iqg, gates = _in_proj(
            x2, mix_norm_g[i].reshape(1, d), w_in[i].astype(BF16), tm=_tile(n, 512))
        ysb = _sb_attn(qkv.reshape(bsz, seq, 3 * sb_width), tq=_tile(seq, 256))
        yhg = _hgrn(fraw.reshape(bsz, seq, hg_width), iqg.reshape(bsz, seq, 3 * hg_width),
                    hgrn_lb_logits, hgrn_out_norm_g[i].reshape(1, hg_width),
                    layer=i, tt=_tile(seq, 512))
        x2 = _merge(x2, ysb.reshape(n, sb_width), yhg.reshape(n, hg_width), gates,
                    w_o_sb[i].astype(BF16), w_o_hg[i].astype(BF16), w_out[i].astype(BF16),
                    tm=_tile(n, 256))
        x2 = _mlp(x2, mlp_norm_g[i].reshape(1, d), w_up[i].astype(BF16), w_down[i].astype(BF16),
                  tm=_tile(n, 512), tf=512)
        x2 = _ple(x2, p[i].reshape(n, p.shape[-1]), w_ple_gate[i].astype(BF16),
                  w_ple_proj[i].astype(BF16), final_norm_g.reshape(1, d),
                  tm=_tile(n, 512), final_norm=(i == depth - 1))
    return x2.reshape(bsz, seq, d)
```

```python
import functools

import numpy as np
import jax
import jax.numpy as jnp
from jax import lax
from jax.experimental import pallas as pl
from jax.experimental.pallas import tpu as pltpu

EPS = 1e-6
HEAD_DIM = 128
SB_HEADS = 8
HG_HEADS = 8
F32 = jnp.float32
BF16 = jnp.bfloat16
LOG2E = 1.4426950408889634

VMEM_LIMIT_BYTES = 56 * 1024 * 1024

EXP_ZERO_BOUND = -104.0


def _params(sem):
    return pltpu.CompilerParams(dimension_semantics=sem, vmem_limit_bytes=VMEM_LIMIT_BYTES)


def _sigmoid(x):
    return 1.0 / (1.0 + jnp.exp(-x))


def _rms_scale(x):
    return lax.rsqrt(jnp.mean(x * x, axis=-1, keepdims=True) + EPS)


def _dot(a, b):
    return jnp.dot(a, b, preferred_element_type=F32)


def _dot_nt(a, b):
    return lax.dot_general(a, b, (((1,), (1,)), ((), ())), preferred_element_type=F32)


def _dot_tn(a, b):
    return lax.dot_general(a, b, (((0,), (0,)), ((), ())), preferred_element_type=F32)


def _in_proj_kernel(x_ref, g_ref, w_ref, qkv_ref, fraw_ref, iqg_ref, gates_ref, h_scr, *, q_scale):
    j = pl.program_id(1)

    @pl.when(j == 0)
    def _():
        x = x_ref[...]
        h_scr[...] = (x * _rms_scale(x) * g_ref[...]).astype(BF16)

    def proj():
        return _dot(h_scr[...], w_ref[...])

    @pl.when(j == 0)
    def _():
        qkv_ref[...] = (proj() * q_scale).astype(BF16)

    @pl.when((j >= 1) & (j < 3))
    def _():
        qkv_ref[...] = proj().astype(BF16)

    @pl.when(j == 3)
    def _():
        fraw_ref[...] = proj()

    @pl.when((j >= 4) & (j < 7))
    def _():
        iqg_ref[...] = proj().astype(BF16)

    @pl.when(j >= 7)
    def _():
        gates_ref[...] = _sigmoid(proj()).astype(BF16)


def _in_proj(x2, g, w_bf16, *, tm):
    n, d = x2.shape
    width = w_bf16.shape[1]
    tn = 1024
    assert width == 11 * tn and n % tm == 0
    nj = width // tn
    kern = functools.partial(_in_proj_kernel, q_scale=HEAD_DIM ** -0.5)
    return pl.pallas_call(
        kern,
        grid=(n // tm, nj),
        in_specs=[
            pl.BlockSpec((tm, d), lambda i, j: (i, 0)),
            pl.BlockSpec((1, d), lambda i, j: (0, 0)),
            pl.BlockSpec((d, tn), lambda i, j: (0, j)),
        ],
        out_specs=[
            pl.BlockSpec((tm, tn), lambda i, j: (i, jnp.clip(j, 0, 2))),
            pl.BlockSpec((tm, tn), lambda i, j: (i, 0)),
            pl.BlockSpec((tm, tn), lambda i, j: (i, jnp.clip(j - 4, 0, 2))),
            pl.BlockSpec((tm, tn), lambda i, j: (i, jnp.clip(j - 7, 0, 3))),
        ],
        out_shape=[
            jax.ShapeDtypeStruct((n, 3 * tn), BF16),
            jax.ShapeDtypeStruct((n, tn), F32),
            jax.ShapeDtypeStruct((n, 3 * tn), BF16),
            jax.ShapeDtypeStruct((n, 4 * tn), BF16),
        ],
        scratch_shapes=[pltpu.VMEM((tm, d), BF16)],
        compiler_params=_params(("parallel", "arbitrary")),
        name="in_proj",
    )(x2, g, w_bf16)


def _sb_attn_kernel(q_ref, k_ref, v_ref, o_ref, acc_scr, r_scr, *, t, heads):
    i = pl.program_id(2)
    rows = heads * t

    jj = lax.broadcasted_iota(jnp.int32, (t, t), 0)
    ss = lax.broadcasted_iota(jnp.int32, (t, t), 1)
    later_mat = jnp.where(jj > ss, 1.0, 0.0).astype(BF16)

    def block(j, r_old):
        diagonal = r_old is None
        k_start = pl.multiple_of(j * t, t)
        zs = []
        for h in range(heads):
            lanes = slice(h * HEAD_DIM, (h + 1) * HEAD_DIM)
            zs.append(_dot_nt(q_ref[0, :, lanes], k_ref[0, pl.ds(k_start, t), lanes]))
        z = jnp.concatenate(zs, axis=0)
        log_keep = -(jnp.maximum(z, 0.0) + jnp.log(1.0 + jnp.exp(-jnp.abs(z))))
        if diagonal:
            row = lax.broadcasted_iota(jnp.int32, (rows, t), 0) & (t - 1)
            col = lax.broadcasted_iota(jnp.int32, (rows, t), 1)
            mask = col < row
            log_keep = jnp.where(mask, log_keep, 0.0)
        hi = log_keep.astype(BF16)
        lo = (log_keep - hi.astype(F32)).astype(BF16)
        cs2 = _dot(jnp.concatenate([hi, lo], axis=0), later_mat)
        cs = cs2[:rows] + cs2[rows:]
        block_sum = cs[:, 0:1] + log_keep[:, 0:1]
        if diagonal:
            later, r_new = cs, block_sum
        else:
            later, r_new = cs + r_old, r_old + block_sum
        w = jnp.exp(z + log_keep + later)
        if diagonal:
            w = jnp.where(mask, w, 0.0)
        w = w.astype(BF16)
        pv = [_dot(w[h * t:(h + 1) * t], v_ref[0, pl.ds(k_start, t), h * HEAD_DIM:(h + 1) * HEAD_DIM])
              for h in range(heads)]
        return pv, r_new

    def first_two():
        pv_d, r_d = block(i, None)
        pv_o, r_new = block(i - 1, r_d)
        for h in range(heads):
            acc_scr[h] = pv_d[h] + pv_o[h]
        r_scr[...] = r_new
        return jnp.max(r_new)

    def diagonal_only():
        pv_d, r_d = block(i, None)
        for h in range(heads):
            acc_scr[h] = pv_d[h]
        r_scr[...] = r_d
        return jnp.max(r_d)

    r_max0 = lax.cond(i > 0, first_two, diagonal_only)

    def body(carry):
        j, _ = carry
        pv, r_new = block(j, r_scr[...])
        for h in range(heads):
            acc_scr[h] += pv[h]
        r_scr[...] = r_new
        return j - 1, jnp.max(r_new)

    def cond(carry):
        j, r_max = carry
        return (j >= 0) & (r_max > EXP_ZERO_BOUND)

    lax.while_loop(cond, body, (i - 2, r_max0))

    for h in range(heads):
        o_ref[0, :, h * HEAD_DIM:(h + 1) * HEAD_DIM] = acc_scr[h].astype(o_ref.dtype)


def _sb_attn(qkv, *, t):
    b, s, w3 = qkv.shape
    width = w3 // 3
    heads = 2
    gw = heads * HEAD_DIM
    ng = width // gw
    assert s % t == 0 and t & (t - 1) == 0
    kern = functools.partial(_sb_attn_kernel, t=t, heads=heads)
    return pl.pallas_call(
        kern,
        grid=(b, ng, s // t),
        in_specs=[
            pl.BlockSpec((1, t, gw), lambda bi, g, i: (bi, i, g)),
            pl.BlockSpec((1, s, gw), lambda bi, g, i: (bi, 0, ng + g)),
            pl.BlockSpec((1, s, gw), lambda bi, g, i: (bi, 0, 2 * ng + g)),
        ],
        out_specs=pl.BlockSpec((1, t, gw), lambda bi, g, i: (bi, i, g)),
        out_shape=jax.ShapeDtypeStruct((b, s, width), BF16),
        scratch_shapes=[
            pltpu.VMEM((heads, t, HEAD_DIM), F32),
            pltpu.VMEM((heads * t, 1), F32),
        ],
        compiler_params=_params(("parallel", "parallel", "arbitrary")),
        name="sb_attn",
    )(qkv, qkv, qkv)


HG_CHUNK = 128
HG_SUB = 8
HG_LEVELS = (8, 16, 32, 64)


def _hgrn_constants():
    C, c = HG_CHUNK, HG_SUB
    t = np.arange(C)[:, None]
    s = np.arange(C)[None, :]
    x = t ^ s
    tri = (s <= t).astype(np.float32)
    masks = [((x < c) & (s <= t)).astype(np.float32)]
    masks += [((t > s) & (x >= m) & (x < 2 * m)).astype(np.float32) for m in HG_LEVELS]
    r = np.arange(c * HEAD_DIM)[:, None]
    j = np.arange(C)[None, :]
    sel = ((j % c) == (r // HEAD_DIM)).astype(np.float32)
    return jnp.asarray(tri, BF16), jnp.asarray(sel, BF16), jnp.asarray(np.stack(masks), F32)


def _hgrn_kernel(fraw_ref, i_ref, q_ref, g_ref, lbl_ref, gn_ref, tri_ref, sel_ref, mask_ref, o_ref,
                 st_scr, b_scr, cb_scr, q_scr, *, layer, chunks):
    C, c = HG_CHUNK, HG_SUB

    @pl.when(pl.program_id(2) == 0)
    def _():
        st_scr[...] = jnp.zeros_like(st_scr)

    logits = lbl_ref[...]
    e = jnp.exp(logits - jnp.max(logits, axis=0, keepdims=True))
    lb = jnp.sum(e[:layer + 1], axis=0, keepdims=True) / jnp.sum(e, axis=0, keepdims=True)
    gn = gn_ref[...]
    sub_row = lax.broadcasted_iota(jnp.int32, (c, HEAD_DIM), 0)
    row_id = lax.broadcasted_iota(jnp.int32, (C, HEAD_DIM), 0)

    def chunk(ci, carry):
        r0 = pl.multiple_of(ci * C, C)
        rows = pl.ds(r0, C)
        f = lb + (1.0 - lb) * _sigmoid(fraw_ref[0, rows, :])
        kk = 1.0 - f
        lf = jnp.log(f) * LOG2E
        p0 = lf.astype(BF16)
        r1 = lf - p0.astype(F32)
        p1 = r1.astype(BF16)
        p2 = (r1 - p1.astype(F32)).astype(BF16)
        tri = tri_ref[...]
        b = _dot(tri, p0) + _dot(tri, p1) + _dot(tri, p2)
        qf = q_ref[0, rows, :].astype(F32)
        qa = qf * _sigmoid(qf)
        v = i_ref[0, rows, :]
        b_scr[...] = b
        cb_scr[...] = b - jnp.log(kk) * LOG2E
        q_scr[...] = qa

        st = st_scr[...]
        o = _dot_nt((qa * jnp.exp2(b)).astype(BF16), st.astype(BF16))

        xs = []
        for s in range(c):
            pieces = []
            for blk in range(C // c):
                lo = blk * c
                d = b_scr[lo:lo + c, :] - cb_scr[lo + s:lo + s + 1, :]
                pieces.append(jnp.where(sub_row >= s, q_scr[lo:lo + c, :] * jnp.exp2(d), 0.0))
            xs.append(jnp.concatenate(pieces, axis=0).astype(BF16))
        p = mask_ref[0] * _dot(jnp.concatenate(xs, axis=1), sel_ref[...])

        for lvl, m in enumerate(HG_LEVELS):
            ref = jnp.concatenate(
                [jnp.broadcast_to(b_scr[blk * 2 * m + m - 1:blk * 2 * m + m, :], (2 * m, HEAD_DIM))
                 for blk in range(C // (2 * m))], axis=0)
            z = (jnp.where((row_id & m) != 0, qa, kk) * jnp.exp2(-jnp.abs(b - ref))).astype(BF16)
            p = p + mask_ref[lvl + 1] * _dot_nt(z, z)
        o = o + _dot(p.astype(BF16), v)

        b_last = b_scr[C - 1:C, :]
        kd = (kk * jnp.exp2(b_last - b)).astype(BF16)
        st_scr[...] = st * jnp.exp2(b_last) + _dot_tn(v, kd)

        gf = g_ref[0, rows, :].astype(F32)
        y = o * _rms_scale(o) * gn * (gf * _sigmoid(gf))
        o_ref[0, rows, :] = y.astype(o_ref.dtype)
        return carry

    lax.fori_loop(0, chunks, chunk, 0, unroll=True)


def _hgrn(fraw, iqg, lb_logits, gnorm, *, layer, tt):
    b, s, width = fraw.shape
    nh = width // HEAD_DIM
    assert s % tt == 0 and tt % HG_CHUNK == 0
    nl = lb_logits.shape[0]
    tri, sel, masks = _hgrn_constants()
    kern = functools.partial(_hgrn_kernel, layer=layer, chunks=tt // HG_CHUNK)
    blk = lambda off: pl.BlockSpec((1, tt, HEAD_DIM), lambda bi, h, t: (bi, t, off + h))
    const = lambda a: pl.BlockSpec(a.shape, lambda bi, h, t: (0,) * a.ndim)
    return pl.pallas_call(
        kern,
        grid=(b, nh, s // tt),
        in_specs=[
            blk(0),
            blk(0),
            blk(nh),
            blk(2 * nh),
            pl.BlockSpec((nl, HEAD_DIM), lambda bi, h, t: (0, h)),
            pl.BlockSpec((1, HEAD_DIM), lambda bi, h, t: (0, h)),
            const(tri), const(sel), const(masks),
        ],
        out_specs=blk(0),
        out_shape=jax.ShapeDtypeStruct((b, s, width), BF16),
        scratch_shapes=[
            pltpu.VMEM((HEAD_DIM, HEAD_DIM), F32),
            pltpu.VMEM((HG_CHUNK, HEAD_DIM), F32),
            pltpu.VMEM((HG_CHUNK, HEAD_DIM), F32),
            pltpu.VMEM((HG_CHUNK, HEAD_DIM), F32),
        ],
        compiler_params=_params(("parallel", "parallel", "arbitrary")),
        name="hgrn",
    )(fraw, iqg, iqg, iqg, lb_logits, gnorm, tri, sel, masks)


def _merge_kernel(x_ref, ysb_ref, yhg_ref, gsb_ref, ghg_ref, wsb_ref, whg_ref, wout_ref, o_ref):
    a = _dot(ysb_ref[...], wsb_ref[...])
    b = _dot(yhg_ref[...], whg_ref[...])
    y = gsb_ref[...].astype(F32) * a + ghg_ref[...].astype(F32) * b
    o_ref[...] = x_ref[...] + _dot(y.astype(BF16), wout_ref[...])


def _const_spec(shape):
    return pl.BlockSpec(shape, lambda i: (0,) * len(shape), pipeline_mode=pl.Buffered(1))


def _merge(x2, ysb, yhg, gates, wsb, whg, wout, *, tm):
    n, d = x2.shape
    wb = ysb.shape[1]
    return pl.pallas_call(
        _merge_kernel,
        grid=(n // tm,),
        in_specs=[
            pl.BlockSpec((tm, d), lambda i: (i, 0)),
            pl.BlockSpec((tm, wb), lambda i: (i, 0)),
            pl.BlockSpec((tm, wb), lambda i: (i, 0)),
            pl.BlockSpec((tm, d), lambda i: (i, 0)),
            pl.BlockSpec((tm, d), lambda i: (i, 1)),
            _const_spec(wsb.shape),
            _const_spec(whg.shape),
            _const_spec(wout.shape),
        ],
        out_specs=pl.BlockSpec((tm, d), lambda i: (i, 0)),
        out_shape=jax.ShapeDtypeStruct((n, d), F32),
        compiler_params=_params(("parallel",)),
        name="merge",
    )(x2, ysb, yhg, gates, gates, wsb, whg, wout)


def _mlp_kernel(x_ref, g_ref, wup_ref, wdn_ref, o_ref, h_scr, acc_scr):
    f = pl.program_id(1)

    @pl.when(f == 0)
    def _():
        x = x_ref[...]
        h_scr[...] = (x * _rms_scale(x) * g_ref[...]).astype(BF16)
        acc_scr[...] = x

    u = jnp.maximum(_dot(h_scr[...], wup_ref[...]), 0.0)
    acc_scr[...] += _dot((u * u).astype(BF16), wdn_ref[...])

    @pl.when(f == pl.num_programs(1) - 1)
    def _():
        o_ref[...] = acc_scr[...]


def _mlp(x2, g, wup, wdn, *, tm, tf):
    n, d = x2.shape
    dff = wup.shape[1]
    return pl.pallas_call(
        _mlp_kernel,
        grid=(n // tm, dff // tf),
        in_specs=[
            pl.BlockSpec((tm, d), lambda i, f: (i, 0)),
            pl.BlockSpec((1, d), lambda i, f: (0, 0)),
            pl.BlockSpec((d, tf), lambda i, f: (0, f)),
            pl.BlockSpec((tf, d), lambda i, f: (f, 0)),
        ],
        out_specs=pl.BlockSpec((tm, d), lambda i, f: (i, 0)),
        out_shape=jax.ShapeDtypeStruct((n, d), F32),
        scratch_shapes=[pltpu.VMEM((tm, d), BF16), pltpu.VMEM((tm, d), F32)],
        compiler_params=_params(("parallel", "arbitrary")),
        name="mlp",
    )(x2, g, wup, wdn)


def _ple_kernel(x_ref, p_ref, wg_ref, wp_ref, fg_ref, o_ref, *, final_norm):
    x = x_ref[...]
    gate = _sigmoid(_dot(x.astype(BF16), wg_ref[...]))
    y = x + gate * _dot(p_ref[...].astype(BF16), wp_ref[...])
    if final_norm:
        y = y * _rms_scale(y) * fg_ref[...]
    o_ref[...] = y


def _ple(x2, p2, wg, wp, fg, *, tm, final_norm):
    n, d = x2.shape
    pd = p2.shape[1]
    kern = functools.partial(_ple_kernel, final_norm=final_norm)
    return pl.pallas_call(
        kern,
        grid=(n // tm,),
        in_specs=[
            pl.BlockSpec((tm, d), lambda i: (i, 0)),
            pl.BlockSpec((tm, pd), lambda i: (i, 0)),
            _const_spec(wg.shape),
            _const_spec(wp.shape),
            _const_spec(fg.shape),
        ],
        out_specs=pl.BlockSpec((tm, d), lambda i: (i, 0)),
        out_shape=jax.ShapeDtypeStruct((n, d), F32),
        compiler_params=_params(("parallel",)),
        name="ple",
    )(x2, p2, wg, wp, fg)


def _tile(n, pref):
    t = min(n, pref)
    assert n % t == 0
    return t


def kernel(x, p, mix_norm_g, w_in, hgrn_lb_logits, hgrn_out_norm_g, w_o_sb, w_o_hg, w_out,
           mlp_norm_g, w_up, w_down, w_ple_proj, w_ple_gate, final_norm_g):
    bsz, seq, d = x.shape
    depth = w_in.shape[0]
    n = bsz * seq
    sb_width = SB_HEADS * HEAD_DIM
    hg_width = HG_HEADS * HEAD_DIM
    x2 = x.reshape(n, d)
    for i in range(depth):
        qkv, fraw, iqg, gates = _in_proj(
            x2, mix_norm_g[i].reshape(1, d), w_in[i].astype(BF16), tm=_tile(n, 512))
        ysb = _sb_attn(qkv.reshape(bsz, seq, 3 * sb_width), t=_tile(seq, 256))
        yhg = _hgrn(fraw.reshape(bsz, seq, hg_width), iqg.reshape(bsz, seq, 3 * hg_width),
                    hgrn_lb_logits, hgrn_out_norm_g[i].reshape(1, hg_width),
                    layer=i, tt=_tile(seq, 512))
        x2 = _merge(x2, ysb.reshape(n, sb_width), yhg.reshape(n, hg_width), gates,
                    w_o_sb[i].astype(BF16), w_o_hg[i].astype(BF16), w_out[i].astype(BF16),
                    tm=_tile(n, 256))
        x2 = _mlp(x2, mlp_norm_g[i].reshape(1, d), w_up[i].astype(BF16), w_down[i].astype(BF16),
                  tm=_tile(n, 512), tf=512)
        x2 = _ple(x2, p[i].reshape(n, p.shape[-1]), w_ple_gate[i].astype(BF16),
                  w_ple_proj[i].astype(BF16), final_norm_g.reshape(1, d),
                  tm=_tile(n, 512), final_norm=(i == depth - 1))
    return x2.reshape(bsz, seq, d)
```

```python
import functools

import numpy as np
import jax
import jax.numpy as jnp
from jax import lax
from jax.experimental import pallas as pl
from jax.experimental.pallas import tpu as pltpu

EPS = 1e-6
HEAD_DIM = 128
SB_HEADS = 8
HG_HEADS = 8
F32 = jnp.float32
BF16 = jnp.bfloat16
LOG2E = 1.4426950408889634

VMEM_LIMIT_BYTES = 56 * 1024 * 1024

EXP2_ZERO_BOUND = -151.0


def _params(sem):
    return pltpu.CompilerParams(dimension_semantics=sem, vmem_limit_bytes=VMEM_LIMIT_BYTES)


def _sigmoid(x):
    return 0.5 * jnp.tanh(0.5 * x) + 0.5


def _neg_abs(x):
    return -jnp.abs(x)


def _rms_scale(x):
    return lax.rsqrt(jnp.mean(x * x, axis=-1, keepdims=True) + EPS)


def _dot(a, b):
    return jnp.dot(a, b, preferred_element_type=F32)


def _dot_nt(a, b):
    return lax.dot_general(a, b, (((1,), (1,)), ((), ())), preferred_element_type=F32)


def _dot_tn(a, b):
    return lax.dot_general(a, b, (((0,), (0,)), ((), ())), preferred_element_type=F32)


def _in_proj_kernel(x_ref, g_ref, w_ref, qkv_ref, fraw_ref, iqg_ref, gates_ref, h_scr, *, q_scale):
    j = pl.program_id(1)

    @pl.when(j == 0)
    def _():
        x = x_ref[...]
        h_scr[...] = (x * _rms_scale(x) * g_ref[...]).astype(BF16)

    def proj():
        return _dot(h_scr[...], w_ref[...])

    @pl.when(j == 0)
    def _():
        qkv_ref[...] = (proj() * q_scale).astype(BF16)

    @pl.when((j >= 1) & (j < 3))
    def _():
        qkv_ref[...] = proj().astype(BF16)

    @pl.when(j == 3)
    def _():
        fraw_ref[...] = proj()

    @pl.when((j >= 4) & (j < 7))
    def _():
        iqg_ref[...] = proj().astype(BF16)

    @pl.when(j >= 7)
    def _():
        gates_ref[...] = _sigmoid(proj()).astype(BF16)


def _in_proj(x2, g, w_bf16, *, tm):
    n, d = x2.shape
    width = w_bf16.shape[1]
    tn = 1024
    assert width == 11 * tn and n % tm == 0
    nj = width // tn
    kern = functools.partial(_in_proj_kernel, q_scale=-LOG2E * HEAD_DIM ** -0.5)
    return pl.pallas_call(
        kern,
        grid=(n // tm, nj),
        in_specs=[
            pl.BlockSpec((tm, d), lambda i, j: (i, 0)),
            pl.BlockSpec((1, d), lambda i, j: (0, 0)),
            pl.BlockSpec((d, tn), lambda i, j: (0, j)),
        ],
        out_specs=[
            pl.BlockSpec((tm, tn), lambda i, j: (i, jnp.clip(j, 0, 2))),
            pl.BlockSpec((tm, tn), lambda i, j: (i, 0)),
            pl.BlockSpec((tm, tn), lambda i, j: (i, jnp.clip(j - 4, 0, 2))),
            pl.BlockSpec((tm, tn), lambda i, j: (i, jnp.clip(j - 7, 0, 3))),
        ],
        out_shape=[
            jax.ShapeDtypeStruct((n, 3 * tn), BF16),
            jax.ShapeDtypeStruct((n, tn), F32),
            jax.ShapeDtypeStruct((n, 3 * tn), BF16),
            jax.ShapeDtypeStruct((n, 4 * tn), BF16),
        ],
        scratch_shapes=[pltpu.VMEM((tm, d), BF16)],
        compiler_params=_params(("parallel", "arbitrary")),
        name="in_proj",
    )(x2, g, w_bf16)


def _sb_attn_kernel(q_ref, k_ref, v_ref, o_ref, acc_scr, r_scr, *, t, heads):
    i = pl.program_id(2)
    rows = heads * t

    jj = lax.broadcasted_iota(jnp.int32, (t, t), 0)
    ss = lax.broadcasted_iota(jnp.int32, (t, t), 1)
    later_mat = jnp.where(jj > ss, 1.0, 0.0).astype(BF16)

    def block(j, r_old):
        diagonal = r_old is None
        k_start = pl.multiple_of(j * t, t)
        zs = []
        for h in range(heads):
            lanes = slice(h * HEAD_DIM, (h + 1) * HEAD_DIM)
            zs.append(_dot_nt(q_ref[0, :, lanes], k_ref[0, pl.ds(k_start, t), lanes]))
        zn = jnp.concatenate(zs, axis=0)
        log_keep = jnp.minimum(zn, 0.0) - jnp.log2(1.0 + jnp.exp2(_neg_abs(zn)))
        if diagonal:
            row = lax.broadcasted_iota(jnp.int32, (rows, t), 0) & (t - 1)
            col = lax.broadcasted_iota(jnp.int32, (rows, t), 1)
            mask = col < row
            log_keep = jnp.where(mask, log_keep, 0.0)
        cs = _dot(log_keep.astype(BF16), later_mat)
        block_sum = cs[:, 0:1] + log_keep[:, 0:1]
        if diagonal:
            later, r_new = cs, block_sum
        else:
            later, r_new = cs + r_old, r_old + block_sum
        w = jnp.exp2((log_keep + later) - zn)
        if diagonal:
            w = jnp.where(mask, w, 0.0)
        w = w.astype(BF16)
        pv = [_dot(w[h * t:(h + 1) * t], v_ref[0, pl.ds(k_start, t), h * HEAD_DIM:(h + 1) * HEAD_DIM])
              for h in range(heads)]
        return pv, r_new

    def first_two():
        pv_d, r_d = block(i, None)
        pv_o, r_new = block(i - 1, r_d)
        for h in range(heads):
            acc_scr[h] = pv_d[h] + pv_o[h]
        r_scr[...] = r_new
        return jnp.max(r_new)

    def diagonal_only():
        pv_d, r_d = block(i, None)
        for h in range(heads):
            acc_scr[h] = pv_d[h]
        r_scr[...] = r_d
        return jnp.max(r_d)

    r_max0 = lax.cond(i > 0, first_two, diagonal_only)

    def body(carry):
        j, _ = carry
        pv, r_new = block(j, r_scr[...])
        for h in range(heads):
            acc_scr[h] += pv[h]
        r_scr[...] = r_new
        return j - 1, jnp.max(r_new)

    def cond(carry):
        j, r_max = carry
        return (j >= 0) & (r_max > EXP2_ZERO_BOUND)

    lax.while_loop(cond, body, (i - 2, r_max0))

    for h in range(heads):
        o_ref[0, :, h * HEAD_DIM:(h + 1) * HEAD_DIM] = acc_scr[h].astype(o_ref.dtype)


def _sb_attn(qkv, *, t):
    b, s, w3 = qkv.shape
    width = w3 // 3
    heads = 2
    gw = heads * HEAD_DIM
    ng = width // gw
    assert s % t == 0 and t & (t - 1) == 0
    kern = functools.partial(_sb_attn_kernel, t=t, heads=heads)
    return pl.pallas_call(
        kern,
        grid=(b, ng, s // t),
        in_specs=[
            pl.BlockSpec((1, t, gw), lambda bi, g, i: (bi, i, g)),
            pl.BlockSpec((1, s, gw), lambda bi, g, i: (bi, 0, ng + g)),
            pl.BlockSpec((1, s, gw), lambda bi, g, i: (bi, 0, 2 * ng + g)),
        ],
        out_specs=pl.BlockSpec((1, t, gw), lambda bi, g, i: (bi, i, g)),
        out_shape=jax.ShapeDtypeStruct((b, s, width), BF16),
        scratch_shapes=[
            pltpu.VMEM((heads, t, HEAD_DIM), F32),
            pltpu.VMEM((heads * t, 1), F32),
        ],
        compiler_params=_params(("parallel", "parallel", "arbitrary")),
        name="sb_attn",
    )(qkv, qkv, qkv)


HG_CHUNK = 128
HG_SUB = 8
HG_LEVELS = (8, 16, 32, 64)


def _hgrn_constants():
    C, c = HG_CHUNK, HG_SUB
    t = np.arange(C)[:, None]
    s = np.arange(C)[None, :]
    x = t ^ s
    tri = (s <= t).astype(np.float32)
    masks = [((x < c) & (s <= t)).astype(np.float32)]
    masks += [((t > s) & (x >= m) & (x < 2 * m)).astype(np.float32) for m in HG_LEVELS]
    r = np.arange(c * HEAD_DIM)[:, None]
    j = np.arange(C)[None, :]
    sel = ((j % c) == (r // HEAD_DIM)).astype(np.float32)
    return jnp.asarray(tri, BF16), jnp.asarray(sel, BF16), jnp.asarray(np.stack(masks), F32)


def _hgrn_kernel(fraw_ref, i_ref, q_ref, g_ref, lbl_ref, gn_ref, tri_ref, sel_ref, mask_ref, o_ref,
                 st_scr, b_scr, cb_scr, q_scr, *, layer, chunks):
    C, c = HG_CHUNK, HG_SUB

    @pl.when(pl.program_id(2) == 0)
    def _():
        st_scr[...] = jnp.zeros_like(st_scr)

    logits = lbl_ref[...]
    e = jnp.exp(logits - jnp.max(logits, axis=0, keepdims=True))
    lb = jnp.sum(e[:layer + 1], axis=0, keepdims=True) / jnp.sum(e, axis=0, keepdims=True)
    gn = gn_ref[...]
    sub_row = lax.broadcasted_iota(jnp.int32, (c, HEAD_DIM), 0)
    row_id = lax.broadcasted_iota(jnp.int32, (C, HEAD_DIM), 0)

    def chunk(ci, carry):
        r0 = pl.multiple_of(ci * C, C)
        rows = pl.ds(r0, C)
        f = lb + (1.0 - lb) * _sigmoid(fraw_ref[0, rows, :])
        kk = 1.0 - f
        lf = jnp.log(f) * LOG2E
        p0 = lf.astype(BF16)
        r1 = lf - p0.astype(F32)
        p1 = r1.astype(BF16)
        p2 = (r1 - p1.astype(F32)).astype(BF16)
        tri = tri_ref[...]
        b = _dot(tri, p0) + _dot(tri, p1) + _dot(tri, p2)
        qf = q_ref[0, rows, :].astype(F32)
        qa = qf * _sigmoid(qf)
        v = i_ref[0, rows, :]
        b_scr[...] = b
        cb_scr[...] = b - jnp.log(kk) * LOG2E
        q_scr[...] = qa

        st = st_scr[...]
        o = _dot_nt((qa * jnp.exp2(b)).astype(BF16), st.astype(BF16))

        xs = []
        for s in range(c):
            pieces = []
            for blk in range(C // c):
                lo = blk * c
                d = b_scr[lo:lo + c, :] - cb_scr[lo + s:lo + s + 1, :]
                pieces.append(jnp.where(sub_row >= s, q_scr[lo:lo + c, :] * jnp.exp2(d), 0.0))
            xs.append(jnp.concatenate(pieces, axis=0).astype(BF16))
        p = mask_ref[0] * _dot(jnp.concatenate(xs, axis=1), sel_ref[...])

        for lvl, m in enumerate(HG_LEVELS):
            ref = jnp.concatenate(
                [jnp.broadcast_to(b_scr[blk * 2 * m + m - 1:blk * 2 * m + m, :], (2 * m, HEAD_DIM))
                 for blk in range(C // (2 * m))], axis=0)
            z = (jnp.where((row_id & m) != 0, qa, kk) * jnp.exp2(_neg_abs(b - ref))).astype(BF16)
            p = p + mask_ref[lvl + 1] * _dot_nt(z, z)
        o = o + _dot(p.astype(BF16), v)

        b_last = b_scr[C - 1:C, :]
        kd = (kk * jnp.exp2(b_last - b)).astype(BF16)
        st_scr[...] = st * jnp.exp2(b_last) + _dot_tn(v, kd)

        gf = g_ref[0, rows, :].astype(F32)
        y = o * _rms_scale(o) * gn * (gf * _sigmoid(gf))
        o_ref[0, rows, :] = y.astype(o_ref.dtype)
        return carry

    lax.fori_loop(0, chunks, chunk, 0, unroll=True)


def _hgrn(fraw, iqg, lb_logits, gnorm, *, layer, tt):
    b, s, width = fraw.shape
    nh = width // HEAD_DIM
    assert s % tt == 0 and tt % HG_CHUNK == 0
    nl = lb_logits.shape[0]
    tri, sel, masks = _hgrn_constants()
    kern = functools.partial(_hgrn_kernel, layer=layer, chunks=tt // HG_CHUNK)
    blk = lambda off: pl.BlockSpec((1, tt, HEAD_DIM), lambda bi, h, t: (bi, t, off + h))
    const = lambda a: pl.BlockSpec(a.shape, lambda bi, h, t: (0,) * a.ndim)
    return pl.pallas_call(
        kern,
        grid=(b, nh, s // tt),
        in_specs=[
            blk(0),
            blk(0),
            blk(nh),
            blk(2 * nh),
            pl.BlockSpec((nl, HEAD_DIM), lambda bi, h, t: (0, h)),
            pl.BlockSpec((1, HEAD_DIM), lambda bi, h, t: (0, h)),
            const(tri), const(sel), const(masks),
        ],
        out_specs=blk(0),
        out_shape=jax.ShapeDtypeStruct((b, s, width), BF16),
        scratch_shapes=[
            pltpu.VMEM((HEAD_DIM, HEAD_DIM), F32),
            pltpu.VMEM((HG_CHUNK, HEAD_DIM), F32),
            pltpu.VMEM((HG_CHUNK, HEAD_DIM), F32),
            pltpu.VMEM((HG_CHUNK, HEAD_DIM), F32),
        ],
        compiler_params=_params(("parallel", "parallel", "arbitrary")),
        name="hgrn",
    )(fraw, iqg, iqg, iqg, lb_logits, gnorm, tri, sel, masks)


def _merge_kernel(x_ref, ysb_ref, yhg_ref, gsb_ref, ghg_ref, wsb_ref, whg_ref, wout_ref, o_ref):
    a = _dot(ysb_ref[...], wsb_ref[...])
    b = _dot(yhg_ref[...], whg_ref[...])
    y = gsb_ref[...].astype(F32) * a + ghg_ref[...].astype(F32) * b
    o_ref[...] = x_ref[...] + _dot(y.astype(BF16), wout_ref[...])


def _const_spec(shape):
    return pl.BlockSpec(shape, lambda i: (0,) * len(shape), pipeline_mode=pl.Buffered(1))


def _merge(x2, ysb, yhg, gates, wsb, whg, wout, *, tm):
    n, d = x2.shape
    wb = ysb.shape[1]
    return pl.pallas_call(
        _merge_kernel,
        grid=(n // tm,),
        in_specs=[
            pl.BlockSpec((tm, d), lambda i: (i, 0)),
            pl.BlockSpec((tm, wb), lambda i: (i, 0)),
            pl.BlockSpec((tm, wb), lambda i: (i, 0)),
            pl.BlockSpec((tm, d), lambda i: (i, 0)),
            pl.BlockSpec((tm, d), lambda i: (i, 1)),
            _const_spec(wsb.shape),
            _const_spec(whg.shape),
            _const_spec(wout.shape),
        ],
        out_specs=pl.BlockSpec((tm, d), lambda i: (i, 0)),
        out_shape=jax.ShapeDtypeStruct((n, d), F32),
        compiler_params=_params(("parallel",)),
        name="merge",
    )(x2, ysb, yhg, gates, gates, wsb, whg, wout)


def _mlp_kernel(x_ref, g_ref, wup_ref, wdn_ref, o_ref, h_scr):
    f = pl.program_id(1)

    @pl.when(f == 0)
    def _():
        x = x_ref[...]
        h_scr[...] = (x * _rms_scale(x) * g_ref[...]).astype(BF16)
        o_ref[...] = x

    u = jnp.maximum(_dot(h_scr[...], wup_ref[...]), 0.0)
    o_ref[...] += _dot((u * u).astype(BF16), wdn_ref[...])


def _mlp(x2, g, wup, wdn, *, tm, tf):
    n, d = x2.shape
    dff = wup.shape[1]
    return pl.pallas_call(
        _mlp_kernel,
        grid=(n // tm, dff // tf),
        in_specs=[
            pl.BlockSpec((tm, d), lambda i, f: (i, 0)),
            pl.BlockSpec((1, d), lambda i, f: (0, 0)),
            pl.BlockSpec((d, tf), lambda i, f: (0, f)),
            pl.BlockSpec((tf, d), lambda i, f: (f, 0)),
        ],
        out_specs=pl.BlockSpec((tm, d), lambda i, f: (i, 0)),
        out_shape=jax.ShapeDtypeStruct((n, d), F32),
        scratch_shapes=[pltpu.VMEM((tm, d), BF16)],
        compiler_params=_params(("parallel", "arbitrary")),
        name="mlp",
    )(x2, g, wup, wdn)


def _ple_kernel(x_ref, p_ref, wg_ref, wp_ref, fg_ref, o_ref, *, final_norm):
    x = x_ref[...]
    gate = _sigmoid(_dot(x.astype(BF16), wg_ref[...]))
    y = x + gate * _dot(p_ref[...].astype(BF16), wp_ref[...])
    if final_norm:
        y = y * _rms_scale(y) * fg_ref[...]
    o_ref[...] = y


def _ple(x2, p2, wg, wp, fg, *, tm, final_norm):
    n, d = x2.shape
    pd = p2.shape[1]
    kern = functools.partial(_ple_kernel, final_norm=final_norm)
    return pl.pallas_call(
        kern,
        grid=(n // tm,),
        in_specs=[
            pl.BlockSpec((tm, d), lambda i: (i, 0)),
            pl.BlockSpec((tm, pd), lambda i: (i, 0)),
            _const_spec(wg.shape),
            _const_spec(wp.shape),
            _const_spec(fg.shape),
        ],
        out_specs=pl.BlockSpec((tm, d), lambda i: (i, 0)),
        out_shape=jax.ShapeDtypeStruct((n, d), F32),
        compiler_params=_params(("parallel",)),
        name="ple",
    )(x2, p2, wg, wp, fg)


def _tile(n, pref):
    t = min(n, pref)
    assert n % t == 0
    return t


def kernel(x, p, mix_norm_g, w_in, hgrn_lb_logits, hgrn_out_norm_g, w_o_sb, w_o_hg, w_out,
           mlp_norm_g, w_up, w_down, w_ple_proj, w_ple_gate, final_norm_g):
    bsz, seq, d = x.shape
    depth = w_in.shape[0]
    n = bsz * seq
    sb_width = SB_HEADS * HEAD_DIM
    hg_width = HG_HEADS * HEAD_DIM
    x2 = x.reshape(n, d)
    for i in range(depth):
        qkv, fraw, iqg, gates = _in_proj(
            x2, mix_norm_g[i].reshape(1, d), w_in[i].astype(BF16), tm=_tile(n, 1024))
        ysb = _sb_attn(qkv.reshape(bsz, seq, 3 * sb_width), t=_tile(seq, 256))
        yhg = _hgrn(fraw.reshape(bsz, seq, hg_width), iqg.reshape(bsz, seq, 3 * hg_width),
                    hgrn_lb_logits, hgrn_out_norm_g[i].reshape(1, hg_width),
                    layer=i, tt=_tile(seq, 1024))
        x2 = _merge(x2, ysb.reshape(n, sb_width), yhg.reshape(n, hg_width), gates,
                    w_o_sb[i].astype(BF16), w_o_hg[i].astype(BF16), w_out[i].astype(BF16),
                    tm=_tile(n, 256))
        x2 = _mlp(x2, mlp_norm_g[i].reshape(1, d), w_up[i].astype(BF16), w_down[i].astype(BF16),
                  tm=_tile(n, 1024), tf=512)
        x2 = _ple(x2, p[i].reshape(n, p.shape[-1]), w_ple_gate[i].astype(BF16),
                  w_ple_proj[i].astype(BF16), final_norm_g.reshape(1, d),
                  tm=_tile(n, 512), final_norm=(i == depth - 1))
    return x2.reshape(bsz, seq, d)
```

```python
import functools

import numpy as np
import jax
import jax.numpy as jnp
from jax import lax
from jax.experimental import pallas as pl
from jax.experimental.pallas import tpu as pltpu

EPS = 1e-6
HEAD_DIM = 128
SB_HEADS = 8
HG_HEADS = 8
F32 = jnp.float32
BF16 = jnp.bfloat16
LOG2E = 1.4426950408889634

VMEM_LIMIT_BYTES = 56 * 1024 * 1024

EXP2_ZERO_BOUND = -151.0


def _params(sem):
    return pltpu.CompilerParams(dimension_semantics=sem, vmem_limit_bytes=VMEM_LIMIT_BYTES)


def _sigmoid(x):
    return 0.5 * jnp.tanh(0.5 * x) + 0.5


def _neg_abs(x):
    return -jnp.abs(x)


def _rms_scale(x):
    return lax.rsqrt(jnp.mean(x * x, axis=-1, keepdims=True) + EPS)


def _dot(a, b):
    return jnp.dot(a, b, preferred_element_type=F32)


def _dot_nt(a, b):
    return lax.dot_general(a, b, (((1,), (1,)), ((), ())), preferred_element_type=F32)


def _dot_tn(a, b):
    return lax.dot_general(a, b, (((0,), (0,)), ((), ())), preferred_element_type=F32)


def _in_proj_kernel(x_ref, g_ref, w_ref, qkv_ref, fraw_ref, iqg_ref, gates_ref, h_scr, *, q_scale):
    j = pl.program_id(1)

    @pl.when(j == 0)
    def _():
        x = x_ref[...]
        h_scr[...] = (x * _rms_scale(x) * g_ref[...]).astype(BF16)

    def proj():
        return _dot(h_scr[...], w_ref[...])

    @pl.when(j == 0)
    def _():
        qkv_ref[...] = (proj() * q_scale).astype(BF16)

    @pl.when((j >= 1) & (j < 3))
    def _():
        qkv_ref[...] = proj().astype(BF16)

    @pl.when(j == 3)
    def _():
        fraw_ref[...] = proj()

    @pl.when((j >= 4) & (j < 7))
    def _():
        iqg_ref[...] = proj().astype(BF16)

    @pl.when(j >= 7)
    def _():
        gates_ref[...] = _sigmoid(proj()).astype(BF16)


def _in_proj(x2, g, w_bf16, *, tm):
    n, d = x2.shape
    width = w_bf16.shape[1]
    tn = 1024
    assert width == 11 * tn and n % tm == 0
    nj = width // tn
    kern = functools.partial(_in_proj_kernel, q_scale=-LOG2E * HEAD_DIM ** -0.5)
    return pl.pallas_call(
        kern,
        grid=(n // tm, nj),
        in_specs=[
            pl.BlockSpec((tm, d), lambda i, j: (i, 0)),
            pl.BlockSpec((1, d), lambda i, j: (0, 0)),
            pl.BlockSpec((d, tn), lambda i, j: (0, j)),
        ],
        out_specs=[
            pl.BlockSpec((tm, tn), lambda i, j: (i, jnp.clip(j, 0, 2))),
            pl.BlockSpec((tm, tn), lambda i, j: (i, 0)),
            pl.BlockSpec((tm, tn), lambda i, j: (i, jnp.clip(j - 4, 0, 2))),
            pl.BlockSpec((tm, tn), lambda i, j: (i, jnp.clip(j - 7, 0, 3))),
        ],
        out_shape=[
            jax.ShapeDtypeStruct((n, 3 * tn), BF16),
            jax.ShapeDtypeStruct((n, tn), F32),
            jax.ShapeDtypeStruct((n, 3 * tn), BF16),
            jax.ShapeDtypeStruct((n, 4 * tn), BF16),
        ],
        scratch_shapes=[pltpu.VMEM((tm, d), BF16)],
        compiler_params=_params(("parallel", "arbitrary")),
        name="in_proj",
    )(x2, g, w_bf16)


def _sb_attn_kernel(q_ref, k_ref, v_ref, o_ref, acc_scr, r_scr, *, t, heads):
    i = pl.program_id(2)
    rows = heads * t

    jj = lax.broadcasted_iota(jnp.int32, (t, t), 0)
    ss = lax.broadcasted_iota(jnp.int32, (t, t), 1)
    later_mat = jnp.where(jj > ss, 1.0, 0.0).astype(BF16)

    def block(j, r_old):
        diagonal = r_old is None
        k_start = pl.multiple_of(j * t, t)
        zs = []
        for h in range(heads):
            lanes = slice(h * HEAD_DIM, (h + 1) * HEAD_DIM)
            zs.append(_dot_nt(q_ref[0, :, lanes], k_ref[0, pl.ds(k_start, t), lanes]))
        zn = jnp.concatenate(zs, axis=0)
        log_keep = jnp.minimum(zn, 0.0) - jnp.log2(1.0 + jnp.exp2(_neg_abs(zn)))
        if diagonal:
            row = lax.broadcasted_iota(jnp.int32, (rows, t), 0) & (t - 1)
            col = lax.broadcasted_iota(jnp.int32, (rows, t), 1)
            mask = col < row
            log_keep = jnp.where(mask, log_keep, 0.0)
        cs = _dot(log_keep.astype(BF16), later_mat)
        block_sum = cs[:, 0:1] + log_keep[:, 0:1]
        if diagonal:
            later, r_new = cs, block_sum
        else:
            later, r_new = cs + r_old, r_old + block_sum
        w = jnp.exp2((log_keep + later) - zn)
        if diagonal:
            w = jnp.where(mask, w, 0.0)
        w = w.astype(BF16)
        pv = [_dot(w[h * t:(h + 1) * t], v_ref[0, pl.ds(k_start, t), h * HEAD_DIM:(h + 1) * HEAD_DIM])
              for h in range(heads)]
        return pv, r_new

    def first_two():
        pv_d, r_d = block(i, None)
        pv_o, r_new = block(i - 1, r_d)
        for h in range(heads):
            acc_scr[h] = pv_d[h] + pv_o[h]
        r_scr[...] = r_new
        return jnp.max(r_new)

    def diagonal_only():
        pv_d, r_d = block(i, None)
        for h in range(heads):
            acc_scr[h] = pv_d[h]
        r_scr[...] = r_d
        return jnp.max(r_d)

    r_max0 = lax.cond(i > 0, first_two, diagonal_only)

    def body(carry):
        j, _ = carry
        pv, r_new = block(j, r_scr[...])
        for h in range(heads):
            acc_scr[h] += pv[h]
        r_scr[...] = r_new
        return j - 1, jnp.max(r_new)

    def cond(carry):
        j, r_max = carry
        return (j >= 0) & (r_max > EXP2_ZERO_BOUND)

    lax.while_loop(cond, body, (i - 2, r_max0))

    for h in range(heads):
        o_ref[0, :, h * HEAD_DIM:(h + 1) * HEAD_DIM] = acc_scr[h].astype(o_ref.dtype)


def _sb_attn(qkv, *, t):
    b, s, w3 = qkv.shape
    width = w3 // 3
    heads = 4
    gw = heads * HEAD_DIM
    ng = width // gw
    assert s % t == 0 and t & (t - 1) == 0
    kern = functools.partial(_sb_attn_kernel, t=t, heads=heads)
    return pl.pallas_call(
        kern,
        grid=(b, ng, s // t),
        in_specs=[
            pl.BlockSpec((1, t, gw), lambda bi, g, i: (bi, i, g)),
            pl.BlockSpec((1, s, gw), lambda bi, g, i: (bi, 0, ng + g), pipeline_mode=pl.Buffered(1)),
            pl.BlockSpec((1, s, gw), lambda bi, g, i: (bi, 0, 2 * ng + g), pipeline_mode=pl.Buffered(1)),
        ],
        out_specs=pl.BlockSpec((1, t, gw), lambda bi, g, i: (bi, i, g)),
        out_shape=jax.ShapeDtypeStruct((b, s, width), BF16),
        scratch_shapes=[
            pltpu.VMEM((heads, t, HEAD_DIM), F32),
            pltpu.VMEM((heads * t, 1), F32),
        ],
        compiler_params=_params(("parallel", "parallel", "arbitrary")),
        name="sb_attn",
    )(qkv, qkv, qkv)


HG_CHUNK = 128
HG_SUB = 8
HG_LEVELS = (8, 16, 32, 64)
HG_MAX_SPREAD = 100.0


def _hgrn_constants():
    C, c = HG_CHUNK, HG_SUB
    t = np.arange(C)[:, None]
    s = np.arange(C)[None, :]
    x = t ^ s
    tri = (s <= t).astype(np.float32)
    masks = [((x < c) & (s <= t)).astype(np.float32)]
    masks += [((t > s) & (x >= m) & (x < 2 * m)).astype(np.float32) for m in HG_LEVELS]
    r = np.arange(c * HEAD_DIM)[:, None]
    j = np.arange(C)[None, :]
    sel = ((j % c) == (r // HEAD_DIM)).astype(np.float32)
    return jnp.asarray(tri, BF16), jnp.asarray(sel, BF16), jnp.asarray(np.stack(masks), F32)


def _hgrn_kernel(fraw_ref, i_ref, q_ref, g_ref, lbl_ref, gn_ref, tri_ref, sel_ref, mask_ref, o_ref,
                 st_scr, b_scr, k_scr, q_scr, cb_scr, *, layer, chunks):
    C, c = HG_CHUNK, HG_SUB

    @pl.when(pl.program_id(2) == 0)
    def _():
        st_scr[...] = jnp.zeros_like(st_scr)

    logits = lbl_ref[...]
    e = jnp.exp(logits - jnp.max(logits, axis=0, keepdims=True))
    lb = jnp.sum(e[:layer + 1], axis=0, keepdims=True) / jnp.sum(e, axis=0, keepdims=True)
    gn = gn_ref[...]
    sub_row = lax.broadcasted_iota(jnp.int32, (c, HEAD_DIM), 0)
    row_id = lax.broadcasted_iota(jnp.int32, (C, HEAD_DIM), 0)

    def block_first_rows(r0):
        return jnp.concatenate(
            [jnp.broadcast_to(b_scr[r0 + blk * c:r0 + blk * c + 1, :], (c, HEAD_DIM))
             for blk in range(C // c)], axis=0)

    spread = jnp.zeros((C, HEAD_DIM), F32)
    for ci in range(chunks):
        r0 = ci * C
        rows = pl.ds(r0, C)
        f = lb + (1.0 - lb) * _sigmoid(fraw_ref[0, rows, :])
        lf = jnp.log2(f)
        p0 = lf.astype(BF16)
        r1 = lf - p0.astype(F32)
        p1 = r1.astype(BF16)
        p2 = (r1 - p1.astype(F32)).astype(BF16)
        tri = tri_ref[...]
        b = _dot(tri, p0) + _dot(tri, p1) + _dot(tri, p2)
        qf = q_ref[0, rows, :].astype(F32)
        b_scr[rows, :] = b
        k_scr[rows, :] = 1.0 - f
        q_scr[rows, :] = qf * _sigmoid(qf)
        spread = jnp.maximum(spread, block_first_rows(r0) - b)
    small_spread = jnp.max(spread) <= HG_MAX_SPREAD

    def block_scores_factored(r0, b, qa, kk):
        d = block_first_rows(r0) - b
        zq = (qa * jnp.exp2(-d)).astype(BF16)
        zk = (kk * jnp.exp2(d)).astype(BF16)
        return _dot_nt(zq, zk)

    def block_scores_explicit(r0, b, qa, kk):
        cb_scr[...] = b - jnp.log2(kk)
        xs = []
        for s in range(c):
            pieces = []
            for blk in range(C // c):
                lo = blk * c
                d = b_scr[r0 + lo:r0 + lo + c, :] - cb_scr[lo + s:lo + s + 1, :]
                pieces.append(jnp.where(sub_row >= s, q_scr[r0 + lo:r0 + lo + c, :] * jnp.exp2(d), 0.0))
            xs.append(jnp.concatenate(pieces, axis=0).astype(BF16))
        return _dot(jnp.concatenate(xs, axis=1), sel_ref[...])

    def finish(block_scores):
        for ci in range(chunks):
            r0 = ci * C
            rows = pl.ds(r0, C)
            b = b_scr[rows, :]
            kk = k_scr[rows, :]
            qa = q_scr[rows, :]
            v = i_ref[0, rows, :]
            st = st_scr[...]
            o = _dot_nt((qa * jnp.exp2(b)).astype(BF16), st.astype(BF16))

            p = mask_ref[0] * block_scores(r0, b, qa, kk)
            for lvl, m in enumerate(HG_LEVELS):
                ref = jnp.concatenate(
                    [jnp.broadcast_to(b_scr[r0 + blk * 2 * m + m - 1:r0 + blk * 2 * m + m, :],
                                      (2 * m, HEAD_DIM)) for blk in range(C // (2 * m))], axis=0)
                z = (jnp.where((row_id & m) != 0, qa, kk) * jnp.exp2(_neg_abs(b - ref))).astype(BF16)
                p = p + mask_ref[lvl + 1] * _dot_nt(z, z)
            o = o + _dot(p.astype(BF16), v)

            b_last = b_scr[r0 + C - 1:r0 + C, :]
            kd = (kk * jnp.exp2(b_last - b)).astype(BF16)
            st_scr[...] = st * jnp.exp2(b_last) + _dot_tn(v, kd)

            gf = g_ref[0, rows, :].astype(F32)
            y = o * _rms_scale(o) * gn * (gf * _sigmoid(gf))
            o_ref[0, rows, :] = y.astype(o_ref.dtype)

    @pl.when(small_spread)
    def _():
        finish(block_scores_factored)

    @pl.when(jnp.logical_not(small_spread))
    def _():
        finish(block_scores_explicit)


def _hgrn(fraw, iqg, lb_logits, gnorm, *, layer, tt):
    b, s, width = fraw.shape
    nh = width // HEAD_DIM
    assert s % tt == 0 and tt % HG_CHUNK == 0
    nl = lb_logits.shape[0]
    tri, sel, masks = _hgrn_constants()
    kern = functools.partial(_hgrn_kernel, layer=layer, chunks=tt // HG_CHUNK)
    blk = lambda off: pl.BlockSpec((1, tt, HEAD_DIM), lambda bi, h, t: (bi, t, off + h))
    const = lambda a: pl.BlockSpec(a.shape, lambda bi, h, t: (0,) * a.ndim)
    return pl.pallas_call(
        kern,
        grid=(b, nh, s // tt),
        in_specs=[
            blk(0),
            blk(0),
            blk(nh),
            blk(2 * nh),
            pl.BlockSpec((nl, HEAD_DIM), lambda bi, h, t: (0, h)),
            pl.BlockSpec((1, HEAD_DIM), lambda bi, h, t: (0, h)),
            const(tri), const(sel), const(masks),
        ],
        out_specs=blk(0),
        out_shape=jax.ShapeDtypeStruct((b, s, width), BF16),
        scratch_shapes=[
            pltpu.VMEM((HEAD_DIM, HEAD_DIM), F32),
            pltpu.VMEM((tt, HEAD_DIM), F32),
            pltpu.VMEM((tt, HEAD_DIM), F32),
            pltpu.VMEM((tt, HEAD_DIM), F32),
            pltpu.VMEM((HG_CHUNK, HEAD_DIM), F32),
        ],
        compiler_params=_params(("parallel", "parallel", "arbitrary")),
        name="hgrn",
    )(fraw, iqg, iqg, iqg, lb_logits, gnorm, tri, sel, masks)


def _merge_kernel(x_ref, ysb_ref, yhg_ref, gsb_ref, ghg_ref, wsb_ref, whg_ref, wout_ref, o_ref):
    a = _dot(ysb_ref[...], wsb_ref[...])
    b = _dot(yhg_ref[...], whg_ref[...])
    y = gsb_ref[...].astype(F32) * a + ghg_ref[...].astype(F32) * b
    o_ref[...] = x_ref[...] + _dot(y.astype(BF16), wout_ref[...])


def _const_spec(shape):
    return pl.BlockSpec(shape, lambda i: (0,) * len(shape), pipeline_mode=pl.Buffered(1))


def _merge(x2, ysb, yhg, gates, wsb, whg, wout, *, tm):
    n, d = x2.shape
    wb = ysb.shape[1]
    return pl.pallas_call(
        _merge_kernel,
        grid=(n // tm,),
        in_specs=[
            pl.BlockSpec((tm, d), lambda i: (i, 0)),
            pl.BlockSpec((tm, wb), lambda i: (i, 0)),
            pl.BlockSpec((tm, wb), lambda i: (i, 0)),
            pl.BlockSpec((tm, d), lambda i: (i, 0)),
            pl.BlockSpec((tm, d), lambda i: (i, 1)),
            _const_spec(wsb.shape),
            _const_spec(whg.shape),
            _const_spec(wout.shape),
        ],
        out_specs=pl.BlockSpec((tm, d), lambda i: (i, 0)),
        out_shape=jax.ShapeDtypeStruct((n, d), F32),
        compiler_params=_params(("parallel",)),
        name="merge",
    )(x2, ysb, yhg, gates, gates, wsb, whg, wout)


def _mlp_kernel(x_ref, g_ref, wup_ref, wdn_ref, o_ref, h_scr):
    f = pl.program_id(1)

    @pl.when(f == 0)
    def _():
        x = x_ref[...]
        h_scr[...] = (x * _rms_scale(x) * g_ref[...]).astype(BF16)
        o_ref[...] = x

    u = jnp.maximum(_dot(h_scr[...], wup_ref[...]), 0.0)
    o_ref[...] += _dot((u * u).astype(BF16), wdn_ref[...])


def _mlp(x2, g, wup, wdn, *, tm, tf):
    n, d = x2.shape
    dff = wup.shape[1]
    return pl.pallas_call(
        _mlp_kernel,
        grid=(n // tm, dff // tf),
        in_specs=[
            pl.BlockSpec((tm, d), lambda i, f: (i, 0)),
            pl.BlockSpec((1, d), lambda i, f: (0, 0)),
            pl.BlockSpec((d, tf), lambda i, f: (0, f)),
            pl.BlockSpec((tf, d), lambda i, f: (f, 0)),
        ],
        out_specs=pl.BlockSpec((tm, d), lambda i, f: (i, 0)),
        out_shape=jax.ShapeDtypeStruct((n, d), F32),
        scratch_shapes=[pltpu.VMEM((tm, d), BF16)],
        compiler_params=_params(("parallel", "arbitrary")),
        name="mlp",
    )(x2, g, wup, wdn)


def _ple_kernel(x_ref, p_ref, wg_ref, wp_ref, fg_ref, o_ref, *, final_norm):
    x = x_ref[...]
    gate = _sigmoid(_dot(x.astype(BF16), wg_ref[...]))
    y = x + gate * _dot(p_ref[...].astype(BF16), wp_ref[...])
    if final_norm:
        y = y * _rms_scale(y) * fg_ref[...]
    o_ref[...] = y


def _ple(x2, p2, wg, wp, fg, *, tm, final_norm):
    n, d = x2.shape
    pd = p2.shape[1]
    kern = functools.partial(_ple_kernel, final_norm=final_norm)
    return pl.pallas_call(
        kern,
        grid=(n // tm,),
        in_specs=[
            pl.BlockSpec((tm, d), lambda i: (i, 0)),
            pl.BlockSpec((tm, pd), lambda i: (i, 0)),
            _const_spec(wg.shape),
            _const_spec(wp.shape),
            _const_spec(fg.shape),
        ],
        out_specs=pl.BlockSpec((tm, d), lambda i: (i, 0)),
        out_shape=jax.ShapeDtypeStruct((n, d), F32),
        compiler_params=_params(("parallel",)),
        name="ple",
    )(x2, p2, wg, wp, fg)


def _tile(n, pref):
    t = min(n, pref)
    assert n % t == 0
    return t


def kernel(x, p, mix_norm_g, w_in, hgrn_lb_logits, hgrn_out_norm_g, w_o_sb, w_o_hg, w_out,
           mlp_norm_g, w_up, w_down, w_ple_proj, w_ple_gate, final_norm_g):
    bsz, seq, d = x.shape
    depth = w_in.shape[0]
    n = bsz * seq
    sb_width = SB_HEADS * HEAD_DIM
    hg_width = HG_HEADS * HEAD_DIM
    x2 = x.reshape(n, d)
    for i in range(depth):
        qkv, fraw, iqg, gates = _in_proj(
            x2, mix_norm_g[i].reshape(1, d), w_in[i].astype(BF16), tm=_tile(n, 1024))
        ysb = _sb_attn(qkv.reshape(bsz, seq, 3 * sb_width), t=_tile(seq, 256))
        yhg = _hgrn(fraw.reshape(bsz, seq, hg_width), iqg.reshape(bsz, seq, 3 * hg_width),
                    hgrn_lb_logits, hgrn_out_norm_g[i].reshape(1, hg_width),
                    layer=i, tt=_tile(seq, 1024))
        x2 = _merge(x2, ysb.reshape(n, sb_width), yhg.reshape(n, hg_width), gates,
                    w_o_sb[i].astype(BF16), w_o_hg[i].astype(BF16), w_out[i].astype(BF16),
                    tm=_tile(n, 256))
        x2 = _mlp(x2, mlp_norm_g[i].reshape(1, d), w_up[i].astype(BF16), w_down[i].astype(BF16),
                  tm=_tile(n, 1024), tf=512)
        x2 = _ple(x2, p[i].reshape(n, p.shape[-1]), w_ple_gate[i].astype(BF16),
                  w_ple_proj[i].astype(BF16), final_norm_g.reshape(1, d),
                  tm=_tile(n, 512), final_norm=(i == depth - 1))
    return x2.reshape(bsz, seq, d)
```

```python
import functools

import numpy as np
import jax
import jax.numpy as jnp
from jax import lax
from jax.experimental import pallas as pl
from jax.experimental.pallas import tpu as pltpu

EPS = 1e-6
HEAD_DIM = 128
SB_HEADS = 8
HG_HEADS = 8
F32 = jnp.float32
BF16 = jnp.bfloat16
LOG2E = 1.4426950408889634

VMEM_LIMIT_BYTES = 56 * 1024 * 1024

EXP2_ZERO_BOUND = -151.0


def _params(sem):
    return pltpu.CompilerParams(dimension_semantics=sem, vmem_limit_bytes=VMEM_LIMIT_BYTES)


def _sigmoid(x):
    return 0.5 * jnp.tanh(0.5 * x) + 0.5


def _neg_abs(x):
    return -jnp.abs(x)


def _rms_scale(x):
    return lax.rsqrt(jnp.mean(x * x, axis=-1, keepdims=True) + EPS)


def _dot(a, b):
    return jnp.dot(a, b, preferred_element_type=F32)


def _dot_nt(a, b):
    return lax.dot_general(a, b, (((1,), (1,)), ((), ())), preferred_element_type=F32)


def _dot_tn(a, b):
    return lax.dot_general(a, b, (((0,), (0,)), ((), ())), preferred_element_type=F32)


def _in_proj_kernel(x_ref, g_ref, w_ref, qkv_ref, fraw_ref, iqg_ref, gates_ref, h_scr, *, q_scale):
    j = pl.program_id(1)

    @pl.when(j == 0)
    def _():
        x = x_ref[...]
        h_scr[...] = (x * _rms_scale(x) * g_ref[...]).astype(BF16)

    def proj():
        return _dot(h_scr[...], w_ref[...])

    @pl.when(j == 0)
    def _():
        qkv_ref[...] = (proj() * q_scale).astype(BF16)

    @pl.when((j >= 1) & (j < 3))
    def _():
        qkv_ref[...] = proj().astype(BF16)

    @pl.when(j == 3)
    def _():
        fraw_ref[...] = proj()

    @pl.when((j >= 4) & (j < 7))
    def _():
        iqg_ref[...] = proj().astype(BF16)

    @pl.when(j >= 7)
    def _():
        gates_ref[...] = _sigmoid(proj()).astype(BF16)


def _in_proj(x2, g, w_bf16, *, tm):
    n, d = x2.shape
    width = w_bf16.shape[1]
    tn = 1024
    assert width == 11 * tn and n % tm == 0
    nj = width // tn
    kern = functools.partial(_in_proj_kernel, q_scale=-LOG2E * HEAD_DIM ** -0.5)
    return pl.pallas_call(
        kern,
        grid=(n // tm, nj),
        in_specs=[
            pl.BlockSpec((tm, d), lambda i, j: (i, 0)),
            pl.BlockSpec((1, d), lambda i, j: (0, 0)),
            pl.BlockSpec((d, tn), lambda i, j: (0, j)),
        ],
        out_specs=[
            pl.BlockSpec((tm, tn), lambda i, j: (i, jnp.clip(j, 0, 2))),
            pl.BlockSpec((tm, tn), lambda i, j: (i, 0)),
            pl.BlockSpec((tm, tn), lambda i, j: (i, jnp.clip(j - 4, 0, 2))),
            pl.BlockSpec((tm, tn), lambda i, j: (i, jnp.clip(j - 7, 0, 3))),
        ],
        out_shape=[
            jax.ShapeDtypeStruct((n, 3 * tn), BF16),
            jax.ShapeDtypeStruct((n, tn), F32),
            jax.ShapeDtypeStruct((n, 3 * tn), BF16),
            jax.ShapeDtypeStruct((n, 4 * tn), BF16),
        ],
        scratch_shapes=[pltpu.VMEM((tm, d), BF16)],
        compiler_params=_params(("parallel", "arbitrary")),
        name="in_proj",
    )(x2, g, w_bf16)


def _sb_attn_kernel(q_ref, k_ref, v_ref, o_ref, acc_scr, r_scr, *, t, heads):
    i = pl.program_id(2)
    rows = heads * t

    jj = lax.broadcasted_iota(jnp.int32, (t, t), 0)
    ss = lax.broadcasted_iota(jnp.int32, (t, t), 1)
    later_mat = jnp.where(jj > ss, 1.0, 0.0).astype(BF16)

    def block(j, r_old):
        diagonal = r_old is None
        k_start = pl.multiple_of(j * t, t)
        zs = []
        for h in range(heads):
            lanes = slice(h * HEAD_DIM, (h + 1) * HEAD_DIM)
            zs.append(_dot_nt(q_ref[0, :, lanes], k_ref[0, pl.ds(k_start, t), lanes]))
        zn = jnp.concatenate(zs, axis=0)
        log_keep = jnp.minimum(zn, 0.0) - jnp.log2(1.0 + jnp.exp2(_neg_abs(zn)))
        if diagonal:
            row = lax.broadcasted_iota(jnp.int32, (rows, t), 0) & (t - 1)
            col = lax.broadcasted_iota(jnp.int32, (rows, t), 1)
            mask = col < row
            log_keep = jnp.where(mask, log_keep, 0.0)
        cs = _dot(log_keep.astype(BF16), later_mat)
        block_sum = cs[:, 0:1] + log_keep[:, 0:1]
        if diagonal:
            later, r_new = cs, block_sum
        else:
            later, r_new = cs + r_old, r_old + block_sum
        w = jnp.exp2((log_keep + later) - zn)
        if diagonal:
            w = jnp.where(mask, w, 0.0)
        w = w.astype(BF16)
        pv = [_dot(w[h * t:(h + 1) * t], v_ref[0, pl.ds(k_start, t), h * HEAD_DIM:(h + 1) * HEAD_DIM])
              for h in range(heads)]
        return pv, r_new

    def first_two():
        pv_d, r_d = block(i, None)
        pv_o, r_new = block(i - 1, r_d)
        for h in range(heads):
            acc_scr[h] = pv_d[h] + pv_o[h]
        r_scr[...] = r_new
        return jnp.max(r_new)

    def diagonal_only():
        pv_d, r_d = block(i, None)
        for h in range(heads):
            acc_scr[h] = pv_d[h]
        r_scr[...] = r_d
        return jnp.max(r_d)

    r_max0 = lax.cond(i > 0, first_two, diagonal_only)

    def body(carry):
        j, _ = carry
        pv, r_new = block(j, r_scr[...])
        for h in range(heads):
            acc_scr[h] += pv[h]
        r_scr[...] = r_new
        return j - 1, jnp.max(r_new)

    def cond(carry):
        j, r_max = carry
        return (j >= 0) & (r_max > EXP2_ZERO_BOUND)

    lax.while_loop(cond, body, (i - 2, r_max0))

    for h in range(heads):
        o_ref[0, :, h * HEAD_DIM:(h + 1) * HEAD_DIM] = acc_scr[h].astype(o_ref.dtype)


def _sb_attn(qkv, *, t):
    b, s, w3 = qkv.shape
    width = w3 // 3
    heads = 4
    gw = heads * HEAD_DIM
    ng = width // gw
    assert s % t == 0 and t & (t - 1) == 0
    kern = functools.partial(_sb_attn_kernel, t=t, heads=heads)
    return pl.pallas_call(
        kern,
        grid=(b, ng, s // t),
        in_specs=[
            pl.BlockSpec((1, t, gw), lambda bi, g, i: (bi, i, g)),
            pl.BlockSpec((1, s, gw), lambda bi, g, i: (bi, 0, ng + g), pipeline_mode=pl.Buffered(1)),
            pl.BlockSpec((1, s, gw), lambda bi, g, i: (bi, 0, 2 * ng + g), pipeline_mode=pl.Buffered(1)),
        ],
        out_specs=pl.BlockSpec((1, t, gw), lambda bi, g, i: (bi, i, g)),
        out_shape=jax.ShapeDtypeStruct((b, s, width), BF16),
        scratch_shapes=[
            pltpu.VMEM((heads, t, HEAD_DIM), F32),
            pltpu.VMEM((heads * t, 1), F32),
        ],
        compiler_params=_params(("parallel", "parallel", "arbitrary")),
        name="sb_attn",
    )(qkv, qkv, qkv)


HG_CHUNK = 128
HG_SUB = 8
HG_LEVELS = (8, 16, 32, 64)
HG_WIDE = 32
HG_MAX_SPREAD = 100.0


def _hgrn_constants():
    C, c = HG_CHUNK, HG_SUB
    t = np.arange(C)[:, None]
    s = np.arange(C)[None, :]
    x = t ^ s
    tri = (s <= t).astype(np.float32)
    masks = [((x < c) & (s <= t)).astype(np.float32)]
    masks += [((t > s) & (x >= m) & (x < 2 * m)).astype(np.float32) for m in HG_LEVELS]
    masks += [((x < HG_WIDE) & (s <= t)).astype(np.float32)]
    r = np.arange(c * HEAD_DIM)[:, None]
    j = np.arange(C)[None, :]
    sel = ((j % c) == (r // HEAD_DIM)).astype(np.float32)
    return jnp.asarray(tri, BF16), jnp.asarray(sel, BF16), jnp.asarray(np.stack(masks), F32)


def _hgrn_kernel(fraw_ref, i_ref, q_ref, g_ref, lbl_ref, gn_ref, tri_ref, sel_ref, mask_ref, o_ref,
                 st_scr, b_scr, k_scr, q_scr, cb_scr, *, layer, chunks):
    C, c = HG_CHUNK, HG_SUB

    @pl.when(pl.program_id(2) == 0)
    def _():
        st_scr[...] = jnp.zeros_like(st_scr)

    logits = lbl_ref[...]
    e = jnp.exp(logits - jnp.max(logits, axis=0, keepdims=True))
    lb = jnp.sum(e[:layer + 1], axis=0, keepdims=True) / jnp.sum(e, axis=0, keepdims=True)
    gn = gn_ref[...]
    sub_row = lax.broadcasted_iota(jnp.int32, (c, HEAD_DIM), 0)
    row_id = lax.broadcasted_iota(jnp.int32, (C, HEAD_DIM), 0)

    def block_first_rows(r0, bs):
        return jnp.concatenate(
            [jnp.broadcast_to(b_scr[r0 + blk * bs:r0 + blk * bs + 1, :], (bs, HEAD_DIM))
             for blk in range(C // bs)], axis=0)

    spread = jnp.zeros((C, HEAD_DIM), F32)
    spread_wide = jnp.zeros((C, HEAD_DIM), F32)
    for ci in range(chunks):
        r0 = ci * C
        rows = pl.ds(r0, C)
        f = lb + (1.0 - lb) * _sigmoid(fraw_ref[0, rows, :])
        lf = jnp.log2(f)
        p0 = lf.astype(BF16)
        r1 = lf - p0.astype(F32)
        p1 = r1.astype(BF16)
        p2 = (r1 - p1.astype(F32)).astype(BF16)
        tri = tri_ref[...]
        b = _dot(tri, p0) + _dot(tri, p1) + _dot(tri, p2)
        qf = q_ref[0, rows, :].astype(F32)
        b_scr[rows, :] = b
        k_scr[rows, :] = 1.0 - f
        q_scr[rows, :] = qf * _sigmoid(qf)
        spread = jnp.maximum(spread, block_first_rows(r0, c) - b)
        spread_wide = jnp.maximum(spread_wide, block_first_rows(r0, HG_WIDE) - b)
    small_spread = jnp.max(spread) <= HG_MAX_SPREAD
    small_spread_wide = jnp.max(spread_wide) <= HG_MAX_SPREAD

    def block_scores_factored(bs, r0, b, qa, kk):
        d = block_first_rows(r0, bs) - b
        zq = (qa * jnp.exp2(-d)).astype(BF16)
        zk = (kk * jnp.exp2(d)).astype(BF16)
        return _dot_nt(zq, zk)

    def block_scores_explicit(r0, b, qa, kk):
        cb_scr[...] = b - jnp.log2(kk)
        xs = []
        for s in range(c):
            pieces = []
            for blk in range(C // c):
                lo = blk * c
                d = b_scr[r0 + lo:r0 + lo + c, :] - cb_scr[lo + s:lo + s + 1, :]
                pieces.append(jnp.where(sub_row >= s, q_scr[r0 + lo:r0 + lo + c, :] * jnp.exp2(d), 0.0))
            xs.append(jnp.concatenate(pieces, axis=0).astype(BF16))
        return _dot(jnp.concatenate(xs, axis=1), sel_ref[...])

    def finish(block_scores, block_mask, first_level):
        for ci in range(chunks):
            r0 = ci * C
            rows = pl.ds(r0, C)
            b = b_scr[rows, :]
            kk = k_scr[rows, :]
            qa = q_scr[rows, :]
            v = i_ref[0, rows, :]
            st = st_scr[...]
            o = _dot_nt((qa * jnp.exp2(b)).astype(BF16), st.astype(BF16))

            p = mask_ref[block_mask] * block_scores(r0, b, qa, kk)
            for lvl, m in enumerate(HG_LEVELS):
                if m < first_level:
                    continue
                ref = jnp.concatenate(
                    [jnp.broadcast_to(b_scr[r0 + blk * 2 * m + m - 1:r0 + blk * 2 * m + m, :],
                                      (2 * m, HEAD_DIM)) for blk in range(C // (2 * m))], axis=0)
                upper = (row_id & m) != 0
                z = (jnp.where(upper, qa, kk)
                     * jnp.exp2((b - ref) * jnp.where(upper, 1.0, -1.0))).astype(BF16)
                p = p + mask_ref[lvl + 1] * _dot_nt(z, z)
            o = o + _dot(p.astype(BF16), v)

            b_last = b_scr[r0 + C - 1:r0 + C, :]
            kd = (kk * jnp.exp2(b_last - b)).astype(BF16)
            st_scr[...] = st * jnp.exp2(b_last) + _dot_tn(v, kd)

            gf = g_ref[0, rows, :].astype(F32)
            y = o * _rms_scale(o) * gn * (gf * _sigmoid(gf))
            o_ref[0, rows, :] = y.astype(o_ref.dtype)

    wide_mask = len(HG_LEVELS) + 1

    @pl.when(small_spread_wide)
    def _():
        finish(functools.partial(block_scores_factored, HG_WIDE), wide_mask, HG_WIDE)

    @pl.when(jnp.logical_not(small_spread_wide) & small_spread)
    def _():
        finish(functools.partial(block_scores_factored, c), 0, c)

    @pl.when(jnp.logical_not(small_spread))
    def _():
        finish(block_scores_explicit, 0, c)


def _hgrn(fraw, iqg, lb_logits, gnorm, *, layer, tt):
    b, s, width = fraw.shape
    nh = width // HEAD_DIM
    assert s % tt == 0 and tt % HG_CHUNK == 0
    nl = lb_logits.shape[0]
    tri, sel, masks = _hgrn_constants()
    kern = functools.partial(_hgrn_kernel, layer=layer, chunks=tt // HG_CHUNK)
    blk = lambda off: pl.BlockSpec((1, tt, HEAD_DIM), lambda bi, h, t: (bi, t, off + h))
    const = lambda a: pl.BlockSpec(a.shape, lambda bi, h, t: (0,) * a.ndim)
    return pl.pallas_call(
        kern,
        grid=(b, nh, s // tt),
        in_specs=[
            blk(0),
            blk(0),
            blk(nh),
            blk(2 * nh),
            pl.BlockSpec((nl, HEAD_DIM), lambda bi, h, t: (0, h)),
            pl.BlockSpec((1, HEAD_DIM), lambda bi, h, t: (0, h)),
            const(tri), const(sel), const(masks),
        ],
        out_specs=blk(0),
        out_shape=jax.ShapeDtypeStruct((b, s, width), BF16),
        scratch_shapes=[
            pltpu.VMEM((HEAD_DIM, HEAD_DIM), F32),
            pltpu.VMEM((tt, HEAD_DIM), F32),
            pltpu.VMEM((tt, HEAD_DIM), F32),
            pltpu.VMEM((tt, HEAD_DIM), F32),
            pltpu.VMEM((HG_CHUNK, HEAD_DIM), F32),
        ],
        compiler_params=_params(("parallel", "parallel", "arbitrary")),
        name="hgrn",
    )(fraw, iqg, iqg, iqg, lb_logits, gnorm, tri, sel, masks)


def _merge_kernel(x_ref, ysb_ref, yhg_ref, gsb_ref, ghg_ref, wsb_ref, whg_ref, wout_ref, o_ref):
    a = _dot(ysb_ref[...], wsb_ref[...])
    b = _dot(yhg_ref[...], whg_ref[...])
    y = gsb_ref[...].astype(F32) * a + ghg_ref[...].astype(F32) * b
    o_ref[...] = x_ref[...] + _dot(y.astype(BF16), wout_ref[...])


def _const_spec(shape):
    return pl.BlockSpec(shape, lambda i: (0,) * len(shape), pipeline_mode=pl.Buffered(1))


def _merge(x2, ysb, yhg, gates, wsb, whg, wout, *, tm):
    n, d = x2.shape
    wb = ysb.shape[1]
    return pl.pallas_call(
        _merge_kernel,
        grid=(n // tm,),
        in_specs=[
            pl.BlockSpec((tm, d), lambda i: (i, 0)),
            pl.BlockSpec((tm, wb), lambda i: (i, 0)),
            pl.BlockSpec((tm, wb), lambda i: (i, 0)),
            pl.BlockSpec((tm, d), lambda i: (i, 0)),
            pl.BlockSpec((tm, d), lambda i: (i, 1)),
            _const_spec(wsb.shape),
            _const_spec(whg.shape),
            _const_spec(wout.shape),
        ],
        out_specs=pl.BlockSpec((tm, d), lambda i: (i, 0)),
        out_shape=jax.ShapeDtypeStruct((n, d), F32),
        compiler_params=_params(("parallel",)),
        name="merge",
    )(x2, ysb, yhg, gates, gates, wsb, whg, wout)


def _mlp_kernel(x_ref, g_ref, wup_ref, wdn_ref, o_ref, h_scr):
    f = pl.program_id(1)

    @pl.when(f == 0)
    def _():
        x = x_ref[...]
        h_scr[...] = (x * _rms_scale(x) * g_ref[...]).astype(BF16)
        o_ref[...] = x

    u = jnp.maximum(_dot(h_scr[...], wup_ref[...]), 0.0)
    o_ref[...] += _dot((u * u).astype(BF16), wdn_ref[...])


def _mlp(x2, g, wup, wdn, *, tm, tf):
    n, d = x2.shape
    dff = wup.shape[1]
    return pl.pallas_call(
        _mlp_kernel,
        grid=(n // tm, dff // tf),
        in_specs=[
            pl.BlockSpec((tm, d), lambda i, f: (i, 0)),
            pl.BlockSpec((1, d), lambda i, f: (0, 0)),
            pl.BlockSpec((d, tf), lambda i, f: (0, f)),
            pl.BlockSpec((tf, d), lambda i, f: (f, 0)),
        ],
        out_specs=pl.BlockSpec((tm, d), lambda i, f: (i, 0)),
        out_shape=jax.ShapeDtypeStruct((n, d), F32),
        scratch_shapes=[pltpu.VMEM((tm, d), BF16)],
        compiler_params=_params(("parallel", "arbitrary")),
        name="mlp",
    )(x2, g, wup, wdn)


def _ple_kernel(x_ref, p_ref, wg_ref, wp_ref, fg_ref, o_ref, *, final_norm):
    x = x_ref[...]
    gate = _sigmoid(_dot(x.astype(BF16), wg_ref[...]))
    y = x + gate * _dot(p_ref[...].astype(BF16), wp_ref[...])
    if final_norm:
        y = y * _rms_scale(y) * fg_ref[...]
    o_ref[...] = y


def _ple(x2, p2, wg, wp, fg, *, tm, final_norm):
    n, d = x2.shape
    pd = p2.shape[1]
    kern = functools.partial(_ple_kernel, final_norm=final_norm)
    return pl.pallas_call(
        kern,
        grid=(n // tm,),
        in_specs=[
            pl.BlockSpec((tm, d), lambda i: (i, 0)),
            pl.BlockSpec((tm, pd), lambda i: (i, 0)),
            _const_spec(wg.shape),
            _const_spec(wp.shape),
            _const_spec(fg.shape),
        ],
        out_specs=pl.BlockSpec((tm, d), lambda i: (i, 0)),
        out_shape=jax.ShapeDtypeStruct((n, d), F32),
        compiler_params=_params(("parallel",)),
        name="ple",
    )(x2, p2, wg, wp, fg)


def _tile(n, pref):
    t = min(n, pref)
    assert n % t == 0
    return t


def kernel(x, p, mix_norm_g, w_in, hgrn_lb_logits, hgrn_out_norm_g, w_o_sb, w_o_hg, w_out,
           mlp_norm_g, w_up, w_down, w_ple_proj, w_ple_gate, final_norm_g):
    bsz, seq, d = x.shape
    depth = w_in.shape[0]
    n = bsz * seq
    sb_width = SB_HEADS * HEAD_DIM
    hg_width = HG_HEADS * HEAD_DIM
    x2 = x.reshape(n, d)
    for i in range(depth):
        qkv, fraw, iqg, gates = _in_proj(
            x2, mix_norm_g[i].reshape(1, d), w_in[i].astype(BF16), tm=_tile(n, 1024))
        ysb = _sb_attn(qkv.reshape(bsz, seq, 3 * sb_width), t=_tile(seq, 256))
        yhg = _hgrn(fraw.reshape(bsz, seq, hg_width), iqg.reshape(bsz, seq, 3 * hg_width),
                    hgrn_lb_logits, hgrn_out_norm_g[i].reshape(1, hg_width),
                    layer=i, tt=_tile(seq, 1024))
        x2 = _merge(x2, ysb.reshape(n, sb_width), yhg.reshape(n, hg_width), gates,
                    w_o_sb[i].astype(BF16), w_o_hg[i].astype(BF16), w_out[i].astype(BF16),
                    tm=_tile(n, 512))
        x2 = _mlp(x2, mlp_norm_g[i].reshape(1, d), w_up[i].astype(BF16), w_down[i].astype(BF16),
                  tm=_tile(n, 1024), tf=512)
        x2 = _ple(x2, p[i].reshape(n, p.shape[-1]), w_ple_gate[i].astype(BF16),
                  w_ple_proj[i].astype(BF16), final_norm_g.reshape(1, d),
                  tm=_tile(n, 512), final_norm=(i == depth - 1))
    return x2.reshape(bsz, seq, d)
```

```python
import functools

import numpy as np
import jax
import jax.numpy as jnp
from jax import lax
from jax.experimental import pallas as pl
from jax.experimental.pallas import tpu as pltpu

EPS = 1e-6
HEAD_DIM = 128
SB_HEADS = 8
HG_HEADS = 8
F32 = jnp.float32
BF16 = jnp.bfloat16
LOG2E = 1.4426950408889634

VMEM_LIMIT_BYTES = 56 * 1024 * 1024

EXP2_ZERO_BOUND = -151.0


def _params(sem):
    return pltpu.CompilerParams(dimension_semantics=sem, vmem_limit_bytes=VMEM_LIMIT_BYTES)


def _sigmoid(x):
    return 0.5 * jnp.tanh(0.5 * x) + 0.5


def _neg_abs(x):
    return -jnp.abs(x)


def _rms_scale(x):
    return lax.rsqrt(jnp.mean(x * x, axis=-1, keepdims=True) + EPS)


def _dot(a, b):
    return jnp.dot(a, b, preferred_element_type=F32)


def _dot_nt(a, b):
    return lax.dot_general(a, b, (((1,), (1,)), ((), ())), preferred_element_type=F32)


def _dot_tn(a, b):
    return lax.dot_general(a, b, (((0,), (0,)), ((), ())), preferred_element_type=F32)


def _in_proj_kernel(x_ref, g_ref, w_ref, scale_ref, o_ref, h_scr, *, n_plain):
    j = pl.program_id(1)

    @pl.when(j == 0)
    def _():
        x = x_ref[...]
        h_scr[...] = (x * _rms_scale(x) * g_ref[...]).astype(BF16)

    def proj():
        return _dot(h_scr[...], w_ref[...])

    @pl.when(j < n_plain)
    def _():
        o_ref[...] = (proj() * scale_ref[...]).astype(o_ref.dtype)

    @pl.when(j >= n_plain)
    def _():
        o_ref[...] = _sigmoid(proj()).astype(o_ref.dtype)


def _in_proj(x2, g, w_bf16, scale, *, tm, tn, n_plain, out_dtype, name):
    n, d = x2.shape
    width = w_bf16.shape[1]
    assert width % tn == 0 and n % tm == 0
    kern = functools.partial(_in_proj_kernel, n_plain=n_plain)
    return pl.pallas_call(
        kern,
        grid=(n // tm, width // tn),
        in_specs=[
            pl.BlockSpec((tm, d), lambda i, j: (i, 0)),
            pl.BlockSpec((1, d), lambda i, j: (0, 0)),
            pl.BlockSpec((d, tn), lambda i, j: (0, j)),
            pl.BlockSpec((1, tn), lambda i, j: (0, j)),
        ],
        out_specs=pl.BlockSpec((tm, tn), lambda i, j: (i, j)),
        out_shape=jax.ShapeDtypeStruct((n, width), out_dtype),
        scratch_shapes=[pltpu.VMEM((tm, d), BF16)],
        compiler_params=_params(("parallel", "arbitrary")),
        name=name,
    )(x2, g, w_bf16, scale)


def _sb_attn_kernel(q_ref, k_ref, v_ref, o_ref, acc_scr, r_scr, *, t, heads):
    i = pl.program_id(2)
    rows = heads * t

    jj = lax.broadcasted_iota(jnp.int32, (t, t), 0)
    ss = lax.broadcasted_iota(jnp.int32, (t, t), 1)
    later_mat = jnp.where(jj > ss, 1.0, 0.0).astype(BF16)

    def block(j, r_old):
        diagonal = r_old is None
        k_start = pl.multiple_of(j * t, t)
        zs = []
        for h in range(heads):
            lanes = slice(h * HEAD_DIM, (h + 1) * HEAD_DIM)
            zs.append(_dot_nt(q_ref[0, :, lanes], k_ref[0, pl.ds(k_start, t), lanes]))
        zn = jnp.concatenate(zs, axis=0)
        log_keep = jnp.minimum(zn, 0.0) - jnp.log2(1.0 + jnp.exp2(_neg_abs(zn)))
        if diagonal:
            row = lax.broadcasted_iota(jnp.int32, (rows, t), 0) & (t - 1)
            col = lax.broadcasted_iota(jnp.int32, (rows, t), 1)
            mask = col < row
            log_keep = jnp.where(mask, log_keep, 0.0)
        cs = _dot(log_keep.astype(BF16), later_mat)
        block_sum = cs[:, 0:1] + log_keep[:, 0:1]
        if diagonal:
            later, r_new = cs, block_sum
        else:
            later, r_new = cs + r_old, r_old + block_sum
        w = jnp.exp2((log_keep + later) - zn)
        if diagonal:
            w = jnp.where(mask, w, 0.0)
        w = w.astype(BF16)
        pv = [_dot(w[h * t:(h + 1) * t], v_ref[0, pl.ds(k_start, t), h * HEAD_DIM:(h + 1) * HEAD_DIM])
              for h in range(heads)]
        return pv, r_new

    def first_two():
        pv_d, r_d = block(i, None)
        pv_o, r_new = block(i - 1, r_d)
        for h in range(heads):
            acc_scr[h] = pv_d[h] + pv_o[h]
        r_scr[...] = r_new
        return jnp.max(r_new)

    def diagonal_only():
        pv_d, r_d = block(i, None)
        for h in range(heads):
            acc_scr[h] = pv_d[h]
        r_scr[...] = r_d
        return jnp.max(r_d)

    r_max0 = lax.cond(i > 0, first_two, diagonal_only)

    def body(carry):
        j, _ = carry
        pv, r_new = block(j, r_scr[...])
        for h in range(heads):
            acc_scr[h] += pv[h]
        r_scr[...] = r_new
        return j - 1, jnp.max(r_new)

    def cond(carry):
        j, r_max = carry
        return (j >= 0) & (r_max > EXP2_ZERO_BOUND)

    lax.while_loop(cond, body, (i - 2, r_max0))

    for h in range(heads):
        o_ref[0, :, h * HEAD_DIM:(h + 1) * HEAD_DIM] = acc_scr[h].astype(o_ref.dtype)


def _sb_attn(qkv, *, width, t):
    b, s, _ = qkv.shape
    heads = 4
    gw = heads * HEAD_DIM
    ng = width // gw
    assert s % t == 0 and t & (t - 1) == 0
    kern = functools.partial(_sb_attn_kernel, t=t, heads=heads)
    return pl.pallas_call(
        kern,
        grid=(b, ng, s // t),
        in_specs=[
            pl.BlockSpec((1, t, gw), lambda bi, g, i: (bi, i, g)),
            pl.BlockSpec((1, s, gw), lambda bi, g, i: (bi, 0, ng + g), pipeline_mode=pl.Buffered(1)),
            pl.BlockSpec((1, s, gw), lambda bi, g, i: (bi, 0, 2 * ng + g), pipeline_mode=pl.Buffered(1)),
        ],
        out_specs=pl.BlockSpec((1, t, gw), lambda bi, g, i: (bi, i, g)),
        out_shape=jax.ShapeDtypeStruct((b, s, width), BF16),
        scratch_shapes=[
            pltpu.VMEM((heads, t, HEAD_DIM), F32),
            pltpu.VMEM((heads * t, 1), F32),
        ],
        compiler_params=_params(("parallel", "parallel", "arbitrary")),
        name="sb_attn",
    )(qkv, qkv, qkv)


HG_CHUNK = 128
HG_SUB = 8
HG_LEVELS = (8, 16, 32, 64)
HG_WIDE = 32
HG_MAX_SPREAD = 100.0


def _hgrn_constants():
    C, c = HG_CHUNK, HG_SUB
    t = np.arange(C)[:, None]
    s = np.arange(C)[None, :]
    x = t ^ s
    tri = (s <= t).astype(np.float32)
    masks = [((x < c) & (s <= t)).astype(np.float32)]
    masks += [((t > s) & (x >= m) & (x < 2 * m)).astype(np.float32) for m in HG_LEVELS]
    masks += [((x < HG_WIDE) & (s <= t)).astype(np.float32)]
    r = np.arange(c * HEAD_DIM)[:, None]
    j = np.arange(C)[None, :]
    sel = ((j % c) == (r // HEAD_DIM)).astype(np.float32)
    return jnp.asarray(tri, BF16), jnp.asarray(sel, BF16), jnp.asarray(np.stack(masks), F32)


def _hgrn_kernel(fraw_ref, i_ref, q_ref, g_ref, lbl_ref, gn_ref, tri_ref, sel_ref, mask_ref, o_ref,
                 st_scr, b_scr, k_scr, q_scr, cb_scr, *, layer, chunks):
    C, c = HG_CHUNK, HG_SUB

    @pl.when(pl.program_id(2) == 0)
    def _():
        st_scr[...] = jnp.zeros_like(st_scr)

    logits = lbl_ref[...]
    e = jnp.exp(logits - jnp.max(logits, axis=0, keepdims=True))
    lb = jnp.sum(e[:layer + 1], axis=0, keepdims=True) / jnp.sum(e, axis=0, keepdims=True)
    gn = gn_ref[...]
    sub_row = lax.broadcasted_iota(jnp.int32, (c, HEAD_DIM), 0)
    row_id = lax.broadcasted_iota(jnp.int32, (C, HEAD_DIM), 0)

    def block_first_rows(r0, bs):
        return jnp.concatenate(
            [jnp.broadcast_to(b_scr[r0 + blk * bs:r0 + blk * bs + 1, :], (bs, HEAD_DIM))
             for blk in range(C // bs)], axis=0)

    spread = jnp.zeros((C, HEAD_DIM), F32)
    spread_wide = jnp.zeros((C, HEAD_DIM), F32)
    for ci in range(chunks):
        r0 = ci * C
        rows = pl.ds(r0, C)
        f = lb + (1.0 - lb) * _sigmoid(fraw_ref[0, rows, :])
        lf = jnp.log2(f)
        p0 = lf.astype(BF16)
        r1 = lf - p0.astype(F32)
        p1 = r1.astype(BF16)
        p2 = (r1 - p1.astype(F32)).astype(BF16)
        tri = tri_ref[...]
        b = _dot(tri, p0) + _dot(tri, p1) + _dot(tri, p2)
        qf = q_ref[0, rows, :].astype(F32)
        b_scr[rows, :] = b
        k_scr[rows, :] = 1.0 - f
        q_scr[rows, :] = qf * _sigmoid(qf)
        spread = jnp.maximum(spread, block_first_rows(r0, c) - b)
        spread_wide = jnp.maximum(spread_wide, block_first_rows(r0, HG_WIDE) - b)
    small_spread = jnp.max(spread) <= HG_MAX_SPREAD
    small_spread_wide = jnp.max(spread_wide) <= HG_MAX_SPREAD

    def block_scores_factored(bs, r0, b, qa, kk):
        d = block_first_rows(r0, bs) - b
        zq = (qa * jnp.exp2(-d)).astype(BF16)
        zk = (kk * jnp.exp2(d)).astype(BF16)
        return _dot_nt(zq, zk)

    def block_scores_explicit(r0, b, qa, kk):
        cb_scr[...] = b - jnp.log2(kk)
        xs = []
        for s in range(c):
            pieces = []
            for blk in range(C // c):
                lo = blk * c
                d = b_scr[r0 + lo:r0 + lo + c, :] - cb_scr[lo + s:lo + s + 1, :]
                pieces.append(jnp.where(sub_row >= s, q_scr[r0 + lo:r0 + lo + c, :] * jnp.exp2(d), 0.0))
            xs.append(jnp.concatenate(pieces, axis=0).astype(BF16))
        return _dot(jnp.concatenate(xs, axis=1), sel_ref[...])

    def finish(block_scores, block_mask, first_level):
        for ci in range(chunks):
            r0 = ci * C
            rows = pl.ds(r0, C)
            b = b_scr[rows, :]
            kk = k_scr[rows, :]
            qa = q_scr[rows, :]
            v = i_ref[0, rows, :]
            st = st_scr[...]
            o = _dot_nt((qa * jnp.exp2(b)).astype(BF16), st.astype(BF16))

            p = mask_ref[block_mask] * block_scores(r0, b, qa, kk)
            for lvl, m in enumerate(HG_LEVELS):
                if m < first_level:
                    continue
                ref = jnp.concatenate(
                    [jnp.broadcast_to(b_scr[r0 + blk * 2 * m + m - 1:r0 + blk * 2 * m + m, :],
                                      (2 * m, HEAD_DIM)) for blk in range(C // (2 * m))], axis=0)
                upper = (row_id & m) != 0
                z = (jnp.where(upper, qa, kk)
                     * jnp.exp2((b - ref) * jnp.where(upper, 1.0, -1.0))).astype(BF16)
                p = p + mask_ref[lvl + 1] * _dot_nt(z, z)
            o = o + _dot(p.astype(BF16), v)

            b_last = b_scr[r0 + C - 1:r0 + C, :]
            kd = (kk * jnp.exp2(b_last - b)).astype(BF16)
            st_scr[...] = st * jnp.exp2(b_last) + _dot_tn(v, kd)

            gf = g_ref[0, rows, :].astype(F32)
            y = o * _rms_scale(o) * gn * (gf * _sigmoid(gf))
            o_ref[0, rows, :] = y.astype(o_ref.dtype)

    wide_mask = len(HG_LEVELS) + 1

    @pl.when(small_spread_wide)
    def _():
        finish(functools.partial(block_scores_factored, HG_WIDE), wide_mask, HG_WIDE)

    @pl.when(jnp.logical_not(small_spread_wide) & small_spread)
    def _():
        finish(functools.partial(block_scores_factored, c), 0, c)

    @pl.when(jnp.logical_not(small_spread))
    def _():
        finish(block_scores_explicit, 0, c)


def _hgrn(fraw, iqg, lb_logits, gnorm, *, iqg_col, layer, tt):
    b, s, width = fraw.shape
    nh = width // HEAD_DIM
    assert s % tt == 0 and tt % HG_CHUNK == 0 and iqg_col % HEAD_DIM == 0
    c0 = iqg_col // HEAD_DIM
    nl = lb_logits.shape[0]
    tri, sel, masks = _hgrn_constants()
    kern = functools.partial(_hgrn_kernel, layer=layer, chunks=tt // HG_CHUNK)
    blk = lambda off: pl.BlockSpec((1, tt, HEAD_DIM), lambda bi, h, t: (bi, t, off + h))
    const = lambda a: pl.BlockSpec(a.shape, lambda bi, h, t: (0,) * a.ndim)
    return pl.pallas_call(
        kern,
        grid=(b, nh, s // tt),
        in_specs=[
            blk(0),
            blk(c0),
            blk(c0 + nh),
            blk(c0 + 2 * nh),
            pl.BlockSpec((nl, HEAD_DIM), lambda bi, h, t: (0, h)),
            pl.BlockSpec((1, HEAD_DIM), lambda bi, h, t: (0, h)),
            const(tri), const(sel), const(masks),
        ],
        out_specs=blk(0),
        out_shape=jax.ShapeDtypeStruct((b, s, width), BF16),
        scratch_shapes=[
            pltpu.VMEM((HEAD_DIM, HEAD_DIM), F32),
            pltpu.VMEM((tt, HEAD_DIM), F32),
            pltpu.VMEM((tt, HEAD_DIM), F32),
            pltpu.VMEM((tt, HEAD_DIM), F32),
            pltpu.VMEM((HG_CHUNK, HEAD_DIM), F32),
        ],
        compiler_params=_params(("parallel", "parallel", "arbitrary")),
        name="hgrn",
    )(fraw, iqg, iqg, iqg, lb_logits, gnorm, tri, sel, masks)


def _merge_kernel(x_ref, ysb_ref, yhg_ref, gsb_ref, ghg_ref, wsb_ref, whg_ref, wout_ref, o_ref):
    a = _dot(ysb_ref[...], wsb_ref[...])
    b = _dot(yhg_ref[...], whg_ref[...])
    y = gsb_ref[...].astype(F32) * a + ghg_ref[...].astype(F32) * b
    o_ref[...] = x_ref[...] + _dot(y.astype(BF16), wout_ref[...])


def _const_spec(shape):
    return pl.BlockSpec(shape, lambda i: (0,) * len(shape), pipeline_mode=pl.Buffered(1))


def _merge(x2, ysb, yhg, gates, wsb, whg, wout, *, gates_col, tm):
    n, d = x2.shape
    wb = ysb.shape[1]
    assert gates_col % d == 0
    g0 = gates_col // d
    return pl.pallas_call(
        _merge_kernel,
        grid=(n // tm,),
        in_specs=[
            pl.BlockSpec((tm, d), lambda i: (i, 0)),
            pl.BlockSpec((tm, wb), lambda i: (i, 0)),
            pl.BlockSpec((tm, wb), lambda i: (i, 0)),
            pl.BlockSpec((tm, d), lambda i: (i, g0)),
            pl.BlockSpec((tm, d), lambda i: (i, g0 + 1)),
            _const_spec(wsb.shape),
            _const_spec(whg.shape),
            _const_spec(wout.shape),
        ],
        out_specs=pl.BlockSpec((tm, d), lambda i: (i, 0)),
        out_shape=jax.ShapeDtypeStruct((n, d), F32),
        compiler_params=_params(("parallel",)),
        name="merge",
    )(x2, ysb, yhg, gates, gates, wsb, whg, wout)


def _mlp_kernel(x_ref, g_ref, wup_ref, wdn_ref, o_ref, h_scr):
    f = pl.program_id(1)

    @pl.when(f == 0)
    def _():
        x = x_ref[...]
        h_scr[...] = (x * _rms_scale(x) * g_ref[...]).astype(BF16)
        o_ref[...] = x

    u = jnp.maximum(_dot(h_scr[...], wup_ref[...]), 0.0)
    o_ref[...] += _dot((u * u).astype(BF16), wdn_ref[...])


def _mlp(x2, g, wup, wdn, *, tm, tf):
    n, d = x2.shape
    dff = wup.shape[1]
    return pl.pallas_call(
        _mlp_kernel,
        grid=(n // tm, dff // tf),
        in_specs=[
            pl.BlockSpec((tm, d), lambda i, f: (i, 0)),
            pl.BlockSpec((1, d), lambda i, f: (0, 0)),
            pl.BlockSpec((d, tf), lambda i, f: (0, f)),
            pl.BlockSpec((tf, d), lambda i, f: (f, 0)),
        ],
        out_specs=pl.BlockSpec((tm, d), lambda i, f: (i, 0)),
        out_shape=jax.ShapeDtypeStruct((n, d), F32),
        scratch_shapes=[pltpu.VMEM((tm, d), BF16)],
        compiler_params=_params(("parallel", "arbitrary")),
        name="mlp",
    )(x2, g, wup, wdn)


def _ple_kernel(x_ref, p_ref, wg_ref, wp_ref, fg_ref, o_ref, *, final_norm):
    x = x_ref[...]
    gate = _sigmoid(_dot(x.astype(BF16), wg_ref[...]))
    y = x + gate * _dot(p_ref[...].astype(BF16), wp_ref[...])
    if final_norm:
        y = y * _rms_scale(y) * fg_ref[...]
    o_ref[...] = y


def _ple(x2, p2, wg, wp, fg, *, tm, final_norm):
    n, d = x2.shape
    pd = p2.shape[1]
    kern = functools.partial(_ple_kernel, final_norm=final_norm)
    return pl.pallas_call(
        kern,
        grid=(n // tm,),
        in_specs=[
            pl.BlockSpec((tm, d), lambda i: (i, 0)),
            pl.BlockSpec((tm, pd), lambda i: (i, 0)),
            _const_spec(wg.shape),
            _const_spec(wp.shape),
            _const_spec(fg.shape),
        ],
        out_specs=pl.BlockSpec((tm, d), lambda i: (i, 0)),
        out_shape=jax.ShapeDtypeStruct((n, d), F32),
        compiler_params=_params(("parallel",)),
        name="ple",
    )(x2, p2, wg, wp, fg)


def _tile(n, pref):
    t = min(n, pref)
    assert n % t == 0
    return t


def kernel(x, p, mix_norm_g, w_in, hgrn_lb_logits, hgrn_out_norm_g, w_o_sb, w_o_hg, w_out,
           mlp_norm_g, w_up, w_down, w_ple_proj, w_ple_gate, final_norm_g):
    bsz, seq, d = x.shape
    depth = w_in.shape[0]
    n = bsz * seq
    sb_width = SB_HEADS * HEAD_DIM
    hg_width = HG_HEADS * HEAD_DIM
    x2 = x.reshape(n, d)
    for i in range(depth):
        f_lo, f_hi = 3 * sb_width, 3 * sb_width + hg_width
        w_act = jnp.concatenate([w_in[i][:, :f_lo], w_in[i][:, f_hi:]], axis=1).astype(BF16)
        act_width = w_act.shape[1]
        gates_col = act_width - 2 * d
        scale = jnp.ones((1, act_width), F32).at[:, :sb_width].set(-LOG2E * HEAD_DIM ** -0.5)
        g_mix = mix_norm_g[i].reshape(1, d)
        act = _in_proj(x2, g_mix, w_act, scale, tm=_tile(n, 1024), tn=d,
                       n_plain=gates_col // d, out_dtype=BF16, name="in_proj")
        fraw = _in_proj(x2, g_mix, w_in[i][:, f_lo:f_hi].astype(BF16), jnp.ones((1, hg_width), F32),
                        tm=_tile(n, 1024), tn=hg_width, n_plain=1, out_dtype=F32, name="in_proj_f")
        act3 = act.reshape(bsz, seq, act_width)
        ysb = _sb_attn(act3, width=sb_width, t=_tile(seq, 256))
        yhg = _hgrn(fraw.reshape(bsz, seq, hg_width), act3,
                    hgrn_lb_logits, hgrn_out_norm_g[i].reshape(1, hg_width),
                    iqg_col=f_lo, layer=i, tt=_tile(seq, 2048))
        x2 = _merge(x2, ysb.reshape(n, sb_width), yhg.reshape(n, hg_width), act,
                    w_o_sb[i].astype(BF16), w_o_hg[i].astype(BF16), w_out[i].astype(BF16),
                    gates_col=gates_col, tm=_tile(n, 512))
        x2 = _mlp(x2, mlp_norm_g[i].reshape(1, d), w_up[i].astype(BF16), w_down[i].astype(BF16),
                  tm=_tile(n, 1024), tf=512)
        x2 = _ple(x2, p[i].reshape(n, p.shape[-1]), w_ple_gate[i].astype(BF16),
                  w_ple_proj[i].astype(BF16), final_norm_g.reshape(1, d),
                  tm=_tile(n, 512), final_norm=(i == depth - 1))
    return x2.reshape(bsz, seq, d)
```

```python
import functools

import numpy as np
import jax
import jax.numpy as jnp
from jax import lax
from jax.experimental import pallas as pl
from jax.experimental.pallas import tpu as pltpu

EPS = 1e-6
HEAD_DIM = 128
SB_HEADS = 8
HG_HEADS = 8
F32 = jnp.float32
BF16 = jnp.bfloat16
LOG2E = 1.4426950408889634

VMEM_LIMIT_BYTES = 56 * 1024 * 1024

EXP2_ZERO_BOUND = -151.0


def _params(sem):
    return pltpu.CompilerParams(dimension_semantics=sem, vmem_limit_bytes=VMEM_LIMIT_BYTES)


def _sigmoid(x):
    return 0.5 * jnp.tanh(0.5 * x) + 0.5


def _neg_abs(x):
    return -jnp.abs(x)


def _rms_scale(x):
    return lax.rsqrt(jnp.mean(x * x, axis=-1, keepdims=True) + EPS)


def _dot(a, b):
    return jnp.dot(a, b, preferred_element_type=F32)


def _dot_nt(a, b):
    return lax.dot_general(a, b, (((1,), (1,)), ((), ())), preferred_element_type=F32)


def _dot_tn(a, b):
    return lax.dot_general(a, b, (((0,), (0,)), ((), ())), preferred_element_type=F32)


def _in_proj_kernel(x_ref, g_ref, *rest, n_w, n_plain):
    w_refs, (scale_ref, o_ref, h_scr) = rest[:n_w], rest[n_w:]
    j = pl.program_id(1)
    tb = o_ref.shape[1] // n_w

    @pl.when(j == 0)
    def _():
        x = x_ref[...]
        h_scr[...] = (x * _rms_scale(x) * g_ref[...]).astype(BF16)

    def store(finish):
        for b, w_ref in enumerate(w_refs):
            cols = slice(b * tb, (b + 1) * tb)
            o_ref[:, cols] = finish(_dot(h_scr[...], w_ref[...]), cols).astype(o_ref.dtype)

    @pl.when(j < n_plain)
    def _():
        store(lambda acc, cols: acc * scale_ref[:, cols])

    @pl.when(j >= n_plain)
    def _():
        store(lambda acc, cols: _sigmoid(acc))


def _in_proj(x2, g, w_bf16, scale, col_blocks, *, tm, tb, steps, n_plain, out_dtype, name):
    n, d = x2.shape
    assert n % tm == 0 and w_bf16.shape[1] % tb == 0
    n_w = len(col_blocks)
    tn = n_w * tb
    kern = functools.partial(_in_proj_kernel, n_w=n_w, n_plain=n_plain)
    w_specs = [pl.BlockSpec((d, tb), lambda i, j, f=f: (0, f(j))) for f in col_blocks]
    return pl.pallas_call(
        kern,
        grid=(n // tm, steps),
        in_specs=[
            pl.BlockSpec((tm, d), lambda i, j: (i, 0)),
            pl.BlockSpec((1, d), lambda i, j: (0, 0)),
            *w_specs,
            pl.BlockSpec((1, tn), lambda i, j: (0, j)),
        ],
        out_specs=pl.BlockSpec((tm, tn), lambda i, j: (i, j)),
        out_shape=jax.ShapeDtypeStruct((n, steps * tn), out_dtype),
        scratch_shapes=[pltpu.VMEM((tm, d), BF16)],
        compiler_params=_params(("parallel", "arbitrary")),
        name=name,
    )(x2, g, *([w_bf16] * n_w), scale)


def _sb_attn_kernel(q_ref, k_ref, v_ref, o_ref, acc_scr, r_scr, *, t, heads):
    i = pl.program_id(2)
    rows = heads * t

    jj = lax.broadcasted_iota(jnp.int32, (t, t), 0)
    ss = lax.broadcasted_iota(jnp.int32, (t, t), 1)
    later_mat = jnp.where(jj > ss, 1.0, 0.0).astype(BF16)

    def block(j, r_old):
        diagonal = r_old is None
        k_start = pl.multiple_of(j * t, t)
        zs = []
        for h in range(heads):
            lanes = slice(h * HEAD_DIM, (h + 1) * HEAD_DIM)
            zs.append(_dot_nt(q_ref[0, :, lanes], k_ref[0, pl.ds(k_start, t), lanes]))
        zn = jnp.concatenate(zs, axis=0)
        log_keep = jnp.minimum(zn, 0.0) - jnp.log2(1.0 + jnp.exp2(_neg_abs(zn)))
        if diagonal:
            row = lax.broadcasted_iota(jnp.int32, (rows, t), 0) & (t - 1)
            col = lax.broadcasted_iota(jnp.int32, (rows, t), 1)
            mask = col < row
            log_keep = jnp.where(mask, log_keep, 0.0)
        cs = _dot(log_keep.astype(BF16), later_mat)
        block_sum = cs[:, 0:1] + log_keep[:, 0:1]
        if diagonal:
            later, r_new = cs, block_sum
        else:
            later, r_new = cs + r_old, r_old + block_sum
        w = jnp.exp2((log_keep + later) - zn)
        if diagonal:
            w = jnp.where(mask, w, 0.0)
        w = w.astype(BF16)
        pv = [_dot(w[h * t:(h + 1) * t], v_ref[0, pl.ds(k_start, t), h * HEAD_DIM:(h + 1) * HEAD_DIM])
              for h in range(heads)]
        return pv, r_new

    def first_two():
        pv_d, r_d = block(i, None)
        pv_o, r_new = block(i - 1, r_d)
        for h in range(heads):
            acc_scr[h] = pv_d[h] + pv_o[h]
        r_scr[...] = r_new
        return jnp.max(r_new)

    def diagonal_only():
        pv_d, r_d = block(i, None)
        for h in range(heads):
            acc_scr[h] = pv_d[h]
        r_scr[...] = r_d
        return jnp.max(r_d)

    r_max0 = lax.cond(i > 0, first_two, diagonal_only)

    def body(carry):
        j, _ = carry
        pv, r_new = block(j, r_scr[...])
        for h in range(heads):
            acc_scr[h] += pv[h]
        r_scr[...] = r_new
        return j - 1, jnp.max(r_new)

    def cond(carry):
        j, r_max = carry
        return (j >= 0) & (r_max > EXP2_ZERO_BOUND)

    lax.while_loop(cond, body, (i - 2, r_max0))

    for h in range(heads):
        o_ref[0, :, h * HEAD_DIM:(h + 1) * HEAD_DIM] = acc_scr[h].astype(o_ref.dtype)


def _sb_attn(qkv, *, width, t):
    b, s, _ = qkv.shape
    heads = 4
    gw = heads * HEAD_DIM
    ng = width // gw
    assert s % t == 0 and t & (t - 1) == 0
    kern = functools.partial(_sb_attn_kernel, t=t, heads=heads)
    return pl.pallas_call(
        kern,
        grid=(b, ng, s // t),
        in_specs=[
            pl.BlockSpec((1, t, gw), lambda bi, g, i: (bi, i, g)),
            pl.BlockSpec((1, s, gw), lambda bi, g, i: (bi, 0, ng + g), pipeline_mode=pl.Buffered(1)),
            pl.BlockSpec((1, s, gw), lambda bi, g, i: (bi, 0, 2 * ng + g), pipeline_mode=pl.Buffered(1)),
        ],
        out_specs=pl.BlockSpec((1, t, gw), lambda bi, g, i: (bi, i, g)),
        out_shape=jax.ShapeDtypeStruct((b, s, width), BF16),
        scratch_shapes=[
            pltpu.VMEM((heads, t, HEAD_DIM), F32),
            pltpu.VMEM((heads * t, 1), F32),
        ],
        compiler_params=_params(("parallel", "parallel", "arbitrary")),
        name="sb_attn",
    )(qkv, qkv, qkv)


HG_CHUNK = 128
HG_SUB = 8
HG_LEVELS = (8, 16, 32, 64)
HG_WIDE = 32
HG_MAX_SPREAD = 100.0


def _hgrn_constants():
    C, c = HG_CHUNK, HG_SUB
    t = np.arange(C)[:, None]
    s = np.arange(C)[None, :]
    x = t ^ s
    tri = (s <= t).astype(np.float32)
    masks = [((x < c) & (s <= t)).astype(np.float32)]
    masks += [((t > s) & (x >= m) & (x < 2 * m)).astype(np.float32) for m in HG_LEVELS]
    masks += [((x < HG_WIDE) & (s <= t)).astype(np.float32)]
    r = np.arange(c * HEAD_DIM)[:, None]
    j = np.arange(C)[None, :]
    sel = ((j % c) == (r // HEAD_DIM)).astype(np.float32)
    return jnp.asarray(tri, BF16), jnp.asarray(sel, BF16), jnp.asarray(np.stack(masks), F32)


def _hgrn_kernel(fraw_ref, i_ref, q_ref, g_ref, lbl_ref, gn_ref, tri_ref, sel_ref, mask_ref, o_ref,
                 st_scr, b_scr, k_scr, q_scr, cb_scr, *, layer, chunks):
    C, c = HG_CHUNK, HG_SUB

    @pl.when(pl.program_id(2) == 0)
    def _():
        st_scr[...] = jnp.zeros_like(st_scr)

    logits = lbl_ref[...]
    e = jnp.exp(logits - jnp.max(logits, axis=0, keepdims=True))
    lb = jnp.sum(e[:layer + 1], axis=0, keepdims=True) / jnp.sum(e, axis=0, keepdims=True)
    gn = gn_ref[...]
    sub_row = lax.broadcasted_iota(jnp.int32, (c, HEAD_DIM), 0)
    row_id = lax.broadcasted_iota(jnp.int32, (C, HEAD_DIM), 0)

    def block_first_rows(r0, bs):
        return jnp.concatenate(
            [jnp.broadcast_to(b_scr[r0 + blk * bs:r0 + blk * bs + 1, :], (bs, HEAD_DIM))
             for blk in range(C // bs)], axis=0)

    spread = jnp.zeros((C, HEAD_DIM), F32)
    spread_wide = jnp.zeros((C, HEAD_DIM), F32)
    for ci in range(chunks):
        r0 = ci * C
        rows = pl.ds(r0, C)
        f = lb + (1.0 - lb) * _sigmoid(fraw_ref[0, rows, :])
        lf = jnp.log2(f)
        p0 = lf.astype(BF16)
        r1 = lf - p0.astype(F32)
        p1 = r1.astype(BF16)
        p2 = (r1 - p1.astype(F32)).astype(BF16)
        tri = tri_ref[...]
        b = _dot(tri, p0) + _dot(tri, p1) + _dot(tri, p2)
        qf = q_ref[0, rows, :].astype(F32)
        b_scr[rows, :] = b
        k_scr[rows, :] = 1.0 - f
        q_scr[rows, :] = qf * _sigmoid(qf)
        spread = jnp.maximum(spread, block_first_rows(r0, c) - b)
        spread_wide = jnp.maximum(spread_wide, block_first_rows(r0, HG_WIDE) - b)
    small_spread = jnp.max(spread) <= HG_MAX_SPREAD
    small_spread_wide = jnp.max(spread_wide) <= HG_MAX_SPREAD

    def block_scores_factored(bs, r0, b, qa, kk):
        d = block_first_rows(r0, bs) - b
        zq = (qa * jnp.exp2(-d)).astype(BF16)
        zk = (kk * jnp.exp2(d)).astype(BF16)
        return _dot_nt(zq, zk)

    def block_scores_explicit(r0, b, qa, kk):
        cb_scr[...] = b - jnp.log2(kk)
        xs = []
        for s in range(c):
            pieces = []
            for blk in range(C // c):
                lo = blk * c
                d = b_scr[r0 + lo:r0 + lo + c, :] - cb_scr[lo + s:lo + s + 1, :]
                pieces.append(jnp.where(sub_row >= s, q_scr[r0 + lo:r0 + lo + c, :] * jnp.exp2(d), 0.0))
            xs.append(jnp.concatenate(pieces, axis=0).astype(BF16))
        return _dot(jnp.concatenate(xs, axis=1), sel_ref[...])

    def finish(block_scores, block_mask, first_level):
        for ci in range(chunks):
            r0 = ci * C
            rows = pl.ds(r0, C)
            b = b_scr[rows, :]
            kk = k_scr[rows, :]
            qa = q_scr[rows, :]
            v = i_ref[0, rows, :]
            st = st_scr[...]
            o = _dot_nt((qa * jnp.exp2(b)).astype(BF16), st.astype(BF16))

            p = mask_ref[block_mask] * block_scores(r0, b, qa, kk)
            for lvl, m in enumerate(HG_LEVELS):
                if m < first_level:
                    continue
                ref = jnp.concatenate(
                    [jnp.broadcast_to(b_scr[r0 + blk * 2 * m + m - 1:r0 + blk * 2 * m + m, :],
                                      (2 * m, HEAD_DIM)) for blk in range(C // (2 * m))], axis=0)
                upper = (row_id & m) != 0
                z = (jnp.where(upper, qa, kk)
                     * jnp.exp2((b - ref) * jnp.where(upper, 1.0, -1.0))).astype(BF16)
                p = p + mask_ref[lvl + 1] * _dot_nt(z, z)
            o = o + _dot(p.astype(BF16), v)

            b_last = b_scr[r0 + C - 1:r0 + C, :]
            kd = (kk * jnp.exp2(b_last - b)).astype(BF16)
            st_scr[...] = st * jnp.exp2(b_last) + _dot_tn(v, kd)

            gf = g_ref[0, rows, :].astype(F32)
            y = o * _rms_scale(o) * gn * (gf * _sigmoid(gf))
            o_ref[0, rows, :] = y.astype(o_ref.dtype)

    wide_mask = len(HG_LEVELS) + 1

    @pl.when(small_spread_wide)
    def _():
        finish(functools.partial(block_scores_factored, HG_WIDE), wide_mask, HG_WIDE)

    @pl.when(jnp.logical_not(small_spread_wide) & small_spread)
    def _():
        finish(functools.partial(block_scores_factored, c), 0, c)

    @pl.when(jnp.logical_not(small_spread))
    def _():
        finish(block_scores_explicit, 0, c)


def _hgrn(fraw, iqg, lb_logits, gnorm, *, iqg_col, layer, tt):
    b, s, width = fraw.shape
    nh = width // HEAD_DIM
    assert s % tt == 0 and tt % HG_CHUNK == 0 and iqg_col % HEAD_DIM == 0
    c0 = iqg_col // HEAD_DIM
    nl = lb_logits.shape[0]
    tri, sel, masks = _hgrn_constants()
    kern = functools.partial(_hgrn_kernel, layer=layer, chunks=tt // HG_CHUNK)
    blk = lambda off: pl.BlockSpec((1, tt, HEAD_DIM), lambda bi, h, t: (bi, t, off + h))
    const = lambda a: pl.BlockSpec(a.shape, lambda bi, h, t: (0,) * a.ndim)
    return pl.pallas_call(
        kern,
        grid=(b, nh, s // tt),
        in_specs=[
            blk(0),
            blk(c0),
            blk(c0 + nh),
            blk(c0 + 2 * nh),
            pl.BlockSpec((nl, HEAD_DIM), lambda bi, h, t: (0, h)),
            pl.BlockSpec((1, HEAD_DIM), lambda bi, h, t: (0, h)),
            const(tri), const(sel), const(masks),
        ],
        out_specs=blk(0),
        out_shape=jax.ShapeDtypeStruct((b, s, width), BF16),
        scratch_shapes=[
            pltpu.VMEM((HEAD_DIM, HEAD_DIM), F32),
            pltpu.VMEM((tt, HEAD_DIM), F32),
            pltpu.VMEM((tt, HEAD_DIM), F32),
            pltpu.VMEM((tt, HEAD_DIM), F32),
            pltpu.VMEM((HG_CHUNK, HEAD_DIM), F32),
        ],
        compiler_params=_params(("parallel", "parallel", "arbitrary")),
        name="hgrn",
    )(fraw, iqg, iqg, iqg, lb_logits, gnorm, tri, sel, masks)


def _merge_kernel(x_ref, ysb_ref, yhg_ref, gsb_ref, ghg_ref, wsb_ref, whg_ref, wout_ref, gn_ref,
                  o_ref, h_ref):
    a = _dot(ysb_ref[...], wsb_ref[...])
    b = _dot(yhg_ref[...], whg_ref[...])
    y = gsb_ref[...].astype(F32) * a + ghg_ref[...].astype(F32) * b
    x1 = x_ref[...] + _dot(y.astype(BF16), wout_ref[...])
    o_ref[...] = x1
    h_ref[...] = (x1 * _rms_scale(x1) * gn_ref[...]).astype(BF16)


def _const_spec(shape):
    return pl.BlockSpec(shape, lambda i: (0,) * len(shape), pipeline_mode=pl.Buffered(1))


def _merge(x2, ysb, yhg, gates, wsb, whg, wout, g_next, *, gates_col, tm):
    n, d = x2.shape
    wb = ysb.shape[1]
    assert gates_col % d == 0
    g0 = gates_col // d
    return pl.pallas_call(
        _merge_kernel,
        grid=(n // tm,),
        in_specs=[
            pl.BlockSpec((tm, d), lambda i: (i, 0)),
            pl.BlockSpec((tm, wb), lambda i: (i, 0)),
            pl.BlockSpec((tm, wb), lambda i: (i, 0)),
            pl.BlockSpec((tm, d), lambda i: (i, g0)),
            pl.BlockSpec((tm, d), lambda i: (i, g0 + 1)),
            _const_spec(wsb.shape),
            _const_spec(whg.shape),
            _const_spec(wout.shape),
            _const_spec(g_next.shape),
        ],
        out_specs=[pl.BlockSpec((tm, d), lambda i: (i, 0)), pl.BlockSpec((tm, d), lambda i: (i, 0))],
        out_shape=[jax.ShapeDtypeStruct((n, d), F32), jax.ShapeDtypeStruct((n, d), BF16)],
        compiler_params=_params(("parallel",)),
        name="merge",
    )(x2, ysb, yhg, gates, gates, wsb, whg, wout, g_next)


def _mlp_kernel(h_ref, wup_ref, wdn_ref, o_ref):
    @pl.when(pl.program_id(1) == 0)
    def _():
        o_ref[...] = jnp.zeros_like(o_ref)

    u = jnp.maximum(_dot(h_ref[...], wup_ref[...]), 0.0)
    o_ref[...] += _dot((u * u).astype(BF16), wdn_ref[...])


def _mlp(h, wup, wdn, *, tm, tf):
    n, d = h.shape
    dff = wup.shape[1]
    return pl.pallas_call(
        _mlp_kernel,
        grid=(n // tm, dff // tf),
        in_specs=[
            pl.BlockSpec((tm, d), lambda i, f: (i, 0)),
            pl.BlockSpec((d, tf), lambda i, f: (0, f)),
            pl.BlockSpec((tf, d), lambda i, f: (f, 0)),
        ],
        out_specs=pl.BlockSpec((tm, d), lambda i, f: (i, 0)),
        out_shape=jax.ShapeDtypeStruct((n, d), F32),
        compiler_params=_params(("parallel", "arbitrary")),
        name="mlp",
    )(h, wup, wdn)


def _ple_kernel(x_ref, m_ref, p_ref, wg_ref, wp_ref, fg_ref, o_ref, *, final_norm):
    x = x_ref[...] + m_ref[...]
    gate = _sigmoid(_dot(x.astype(BF16), wg_ref[...]))
    y = x + gate * _dot(p_ref[...].astype(BF16), wp_ref[...])
    if final_norm:
        y = y * _rms_scale(y) * fg_ref[...]
    o_ref[...] = y


def _ple(x2, m2, p2, wg, wp, fg, *, tm, final_norm):
    n, d = x2.shape
    pd = p2.shape[1]
    kern = functools.partial(_ple_kernel, final_norm=final_norm)
    return pl.pallas_call(
        kern,
        grid=(n // tm,),
        in_specs=[
            pl.BlockSpec((tm, d), lambda i: (i, 0)),
            pl.BlockSpec((tm, d), lambda i: (i, 0)),
            pl.BlockSpec((tm, pd), lambda i: (i, 0)),
            _const_spec(wg.shape),
            _const_spec(wp.shape),
            _const_spec(fg.shape),
        ],
        out_specs=pl.BlockSpec((tm, d), lambda i: (i, 0)),
        out_shape=jax.ShapeDtypeStruct((n, d), F32),
        compiler_params=_params(("parallel",)),
        name="ple",
    )(x2, m2, p2, wg, wp, fg)


def _tile(n, pref):
    t = min(n, pref)
    assert n % t == 0
    return t


def kernel(x, p, mix_norm_g, w_in, hgrn_lb_logits, hgrn_out_norm_g, w_o_sb, w_o_hg, w_out,
           mlp_norm_g, w_up, w_down, w_ple_proj, w_ple_gate, final_norm_g):
    bsz, seq, d = x.shape
    depth = w_in.shape[0]
    n = bsz * seq
    sb_width = SB_HEADS * HEAD_DIM
    hg_width = HG_HEADS * HEAD_DIM
    x2 = x.reshape(n, d)
    for i in range(depth):
        assert sb_width == hg_width and d == 2 * hg_width
        tb = hg_width
        f_lo = 3 * sb_width
        w_bf = w_in[i].astype(BF16)
        act_width = w_bf.shape[1] - hg_width
        gates_col = act_width - 2 * d
        scale = jnp.ones((1, act_width), F32).at[:, :sb_width].set(-LOG2E * HEAD_DIM ** -0.5)
        g_mix = mix_norm_g[i].reshape(1, d)
        act = _in_proj(x2, g_mix, w_bf, scale,
                       [lambda j: 2 * j + jnp.where(j >= 2, 1, 0),
                        lambda j: 2 * j + 1 + jnp.where(j >= 1, 1, 0)],
                       tm=_tile(n, 1024), tb=tb, steps=act_width // d,
                       n_plain=gates_col // d, out_dtype=BF16, name="in_proj")
        fraw = _in_proj(x2, g_mix, w_bf, jnp.ones((1, hg_width), F32), [lambda j: f_lo // tb],
                        tm=_tile(n, 1024), tb=tb, steps=1, n_plain=1, out_dtype=F32,
                        name="in_proj_f")
        act3 = act.reshape(bsz, seq, act_width)
        ysb = _sb_attn(act3, width=sb_width, t=_tile(seq, 256))
        yhg = _hgrn(fraw.reshape(bsz, seq, hg_width), act3,
                    hgrn_lb_logits, hgrn_out_norm_g[i].reshape(1, hg_width),
                    iqg_col=f_lo, layer=i, tt=_tile(seq, 2048))
        x2, h_mlp = _merge(x2, ysb.reshape(n, sb_width), yhg.reshape(n, hg_width), act,
                           w_o_sb[i].astype(BF16), w_o_hg[i].astype(BF16), w_out[i].astype(BF16),
                           mlp_norm_g[i].reshape(1, d), gates_col=gates_col, tm=_tile(n, 256))
        m2 = _mlp(h_mlp, w_up[i].astype(BF16), w_down[i].astype(BF16), tm=_tile(n, 1024), tf=1024)
        x2 = _ple(x2, m2, p[i].reshape(n, p.shape[-1]), w_ple_gate[i].astype(BF16),
                  w_ple_proj[i].astype(BF16), final_norm_g.reshape(1, d),
                  tm=_tile(n, 512), final_norm=(i == depth - 1))
    return x2.reshape(bsz, seq, d)
```

```python
import functools

import numpy as np
import jax
import jax.numpy as jnp
from jax import lax
from jax.experimental import pallas as pl
from jax.experimental.pallas import tpu as pltpu

EPS = 1e-6
HEAD_DIM = 128
SB_HEADS = 8
HG_HEADS = 8
F32 = jnp.float32
BF16 = jnp.bfloat16
LOG2E = 1.4426950408889634

VMEM_LIMIT_BYTES = 56 * 1024 * 1024

EXP2_ZERO_BOUND = -151.0


def _params(sem):
    return pltpu.CompilerParams(dimension_semantics=sem, vmem_limit_bytes=VMEM_LIMIT_BYTES)


def _sigmoid(x):
    return 0.5 * jnp.tanh(0.5 * x) + 0.5


def _neg_abs(x):
    return -jnp.abs(x)


def _rms_scale(x):
    return lax.rsqrt(jnp.mean(x * x, axis=-1, keepdims=True) + EPS)


def _dot(a, b):
    return jnp.dot(a, b, preferred_element_type=F32)


def _dot_nt(a, b):
    return lax.dot_general(a, b, (((1,), (1,)), ((), ())), preferred_element_type=F32)


def _dot_tn(a, b):
    return lax.dot_general(a, b, (((0,), (0,)), ((), ())), preferred_element_type=F32)


def _in_proj_kernel(*refs, n_w, n_plain, normalize):
    if normalize:
        x_ref, g_ref, *rest = refs
        h_ref = rest[n_w + 2]
    else:
        x_ref, *rest = refs
        h_ref = x_ref
    w_refs, scale_ref, o_ref = rest[:n_w], rest[n_w], rest[n_w + 1]
    j = pl.program_id(1)
    tb = o_ref.shape[1] // n_w

    if normalize:
        @pl.when(j == 0)
        def _():
            x = x_ref[...]
            h_ref[...] = (x * _rms_scale(x) * g_ref[...]).astype(BF16)

    def store(finish):
        for b, w_ref in enumerate(w_refs):
            cols = slice(b * tb, (b + 1) * tb)
            o_ref[:, cols] = finish(_dot(h_ref[...], w_ref[...]), cols).astype(o_ref.dtype)

    @pl.when(j < n_plain)
    def _():
        store(lambda acc, cols: acc * scale_ref[:, cols])

    @pl.when(j >= n_plain)
    def _():
        store(lambda acc, cols: _sigmoid(acc))


def _in_proj(x2, g, w_bf16, scale, col_blocks, *, tm, tb, steps, n_plain, out_dtype, name):
    n, d = x2.shape
    assert n % tm == 0 and w_bf16.shape[1] % tb == 0
    normalize = g is not None
    n_w = len(col_blocks)
    tn = n_w * tb
    kern = functools.partial(_in_proj_kernel, n_w=n_w, n_plain=n_plain, normalize=normalize)
    row_spec = pl.BlockSpec((tm, d), lambda i, j: (i, 0))
    w_specs = [pl.BlockSpec((d, tb), lambda i, j, f=f: (0, f(j))) for f in col_blocks]
    proj_spec = pl.BlockSpec((tm, tn), lambda i, j: (i, j))
    proj_shape = jax.ShapeDtypeStruct((n, steps * tn), out_dtype)
    gain = [g] if normalize else []
    gain_spec = [pl.BlockSpec((1, d), lambda i, j: (0, 0))] if normalize else []
    return pl.pallas_call(
        kern,
        grid=(n // tm, steps),
        in_specs=[row_spec, *gain_spec, *w_specs, pl.BlockSpec((1, tn), lambda i, j: (0, j))],
        out_specs=[proj_spec, row_spec] if normalize else proj_spec,
        out_shape=[proj_shape, jax.ShapeDtypeStruct((n, d), BF16)] if normalize else proj_shape,
        compiler_params=_params(("parallel", "arbitrary")),
        name=name,
    )(x2, *gain, *([w_bf16] * n_w), scale)


def _sb_attn_kernel(q_ref, k_ref, v_ref, o_ref, acc_scr, r_scr, *, t, heads):
    i = pl.program_id(2)
    rows = heads * t

    jj = lax.broadcasted_iota(jnp.int32, (t, t), 0)
    ss = lax.broadcasted_iota(jnp.int32, (t, t), 1)
    later_mat = jnp.where(jj > ss, 1.0, 0.0).astype(BF16)

    def block(j, r_old):
        diagonal = r_old is None
        k_start = pl.multiple_of(j * t, t)
        zs = []
        for h in range(heads):
            lanes = slice(h * HEAD_DIM, (h + 1) * HEAD_DIM)
            zs.append(_dot_nt(q_ref[0, :, lanes], k_ref[0, pl.ds(k_start, t), lanes]))
        zn = jnp.concatenate(zs, axis=0)
        log_keep = jnp.minimum(zn, 0.0) - jnp.log2(1.0 + jnp.exp2(_neg_abs(zn)))
        if diagonal:
            row = lax.broadcasted_iota(jnp.int32, (rows, t), 0) & (t - 1)
            col = lax.broadcasted_iota(jnp.int32, (rows, t), 1)
            mask = col < row
            log_keep = jnp.where(mask, log_keep, 0.0)
        cs = _dot(log_keep.astype(BF16), later_mat)
        block_sum = cs[:, 0:1] + log_keep[:, 0:1]
        if diagonal:
            later, r_new = cs, block_sum
        else:
            later, r_new = cs + r_old, r_old + block_sum
        w = jnp.exp2((log_keep + later) - zn)
        if diagonal:
            w = jnp.where(mask, w, 0.0)
        w = w.astype(BF16)
        pv = [_dot(w[h * t:(h + 1) * t], v_ref[0, pl.ds(k_start, t), h * HEAD_DIM:(h + 1) * HEAD_DIM])
              for h in range(heads)]
        return pv, r_new

    def first_two():
        pv_d, r_d = block(i, None)
        pv_o, r_new = block(i - 1, r_d)
        for h in range(heads):
            acc_scr[h] = pv_d[h] + pv_o[h]
        r_scr[...] = r_new
        return jnp.max(r_new)

    def diagonal_only():
        pv_d, r_d = block(i, None)
        for h in range(heads):
            acc_scr[h] = pv_d[h]
        r_scr[...] = r_d
        return jnp.max(r_d)

    r_max0 = lax.cond(i > 0, first_two, diagonal_only)

    def body(carry):
        j, _ = carry
        pv, r_new = block(j, r_scr[...])
        for h in range(heads):
            acc_scr[h] += pv[h]
        r_scr[...] = r_new
        return j - 1, jnp.max(r_new)

    def cond(carry):
        j, r_max = carry
        return (j >= 0) & (r_max > EXP2_ZERO_BOUND)

    lax.while_loop(cond, body, (i - 2, r_max0))

    for h in range(heads):
        o_ref[0, :, h * HEAD_DIM:(h + 1) * HEAD_DIM] = acc_scr[h].astype(o_ref.dtype)


def _sb_attn(qkv, *, width, t):
    b, s, _ = qkv.shape
    heads = 4
    gw = heads * HEAD_DIM
    ng = width // gw
    assert s % t == 0 and t & (t - 1) == 0
    kern = functools.partial(_sb_attn_kernel, t=t, heads=heads)
    return pl.pallas_call(
        kern,
        grid=(b, ng, s // t),
        in_specs=[
            pl.BlockSpec((1, t, gw), lambda bi, g, i: (bi, i, g)),
            pl.BlockSpec((1, s, gw), lambda bi, g, i: (bi, 0, ng + g), pipeline_mode=pl.Buffered(1)),
            pl.BlockSpec((1, s, gw), lambda bi, g, i: (bi, 0, 2 * ng + g), pipeline_mode=pl.Buffered(1)),
        ],
        out_specs=pl.BlockSpec((1, t, gw), lambda bi, g, i: (bi, i, g)),
        out_shape=jax.ShapeDtypeStruct((b, s, width), BF16),
        scratch_shapes=[
            pltpu.VMEM((heads, t, HEAD_DIM), F32),
            pltpu.VMEM((heads * t, 1), F32),
        ],
        compiler_params=_params(("parallel", "parallel", "arbitrary")),
        name="sb_attn",
    )(qkv, qkv, qkv)


HG_CHUNK = 128
HG_SUB = 8
HG_LEVELS = (8, 16, 32, 64)
HG_WIDE = 32
HG_MAX_SPREAD = 100.0


def _hgrn_constants():
    C, c = HG_CHUNK, HG_SUB
    t = np.arange(C)[:, None]
    s = np.arange(C)[None, :]
    x = t ^ s
    tri = (s <= t).astype(np.float32)
    masks = [((x < c) & (s <= t)).astype(np.float32)]
    masks += [((t > s) & (x >= m) & (x < 2 * m)).astype(np.float32) for m in HG_LEVELS]
    masks += [((x < HG_WIDE) & (s <= t)).astype(np.float32)]
    r = np.arange(c * HEAD_DIM)[:, None]
    j = np.arange(C)[None, :]
    sel = ((j % c) == (r // HEAD_DIM)).astype(np.float32)
    return jnp.asarray(tri, BF16), jnp.asarray(sel, BF16), jnp.asarray(np.stack(masks), F32)


def _hgrn_kernel(fraw_ref, i_ref, q_ref, g_ref, lbl_ref, gn_ref, tri_ref, sel_ref, mask_ref, o_ref,
                 st_scr, b_scr, k_scr, q_scr, cb_scr, *, layer, chunks):
    C, c = HG_CHUNK, HG_SUB

    @pl.when(pl.program_id(2) == 0)
    def _():
        st_scr[...] = jnp.zeros_like(st_scr)

    logits = lbl_ref[...]
    e = jnp.exp(logits - jnp.max(logits, axis=0, keepdims=True))
    lb = jnp.sum(e[:layer + 1], axis=0, keepdims=True) / jnp.sum(e, axis=0, keepdims=True)
    gn = gn_ref[...]
    sub_row = lax.broadcasted_iota(jnp.int32, (c, HEAD_DIM), 0)
    row_id = lax.broadcasted_iota(jnp.int32, (C, HEAD_DIM), 0)

    def block_first_rows(r0, bs):
        return jnp.concatenate(
            [jnp.broadcast_to(b_scr[r0 + blk * bs:r0 + blk * bs + 1, :], (bs, HEAD_DIM))
             for blk in range(C // bs)], axis=0)

    spread = jnp.zeros((C, HEAD_DIM), F32)
    spread_wide = jnp.zeros((C, HEAD_DIM), F32)
    for ci in range(chunks):
        r0 = ci * C
        rows = pl.ds(r0, C)
        f = lb + (1.0 - lb) * _sigmoid(fraw_ref[0, rows, :])
        lf = jnp.log2(f)
        p0 = lf.astype(BF16)
        r1 = lf - p0.astype(F32)
        p1 = r1.astype(BF16)
        p2 = (r1 - p1.astype(F32)).astype(BF16)
        tri = tri_ref[...]
        b = _dot(tri, p0) + _dot(tri, p1) + _dot(tri, p2)
        qf = q_ref[0, rows, :].astype(F32)
        b_scr[rows, :] = b
        k_scr[rows, :] = 1.0 - f
        q_scr[rows, :] = qf * _sigmoid(qf)
        spread = jnp.maximum(spread, block_first_rows(r0, c) - b)
        spread_wide = jnp.maximum(spread_wide, block_first_rows(r0, HG_WIDE) - b)
    small_spread = jnp.max(spread) <= HG_MAX_SPREAD
    small_spread_wide = jnp.max(spread_wide) <= HG_MAX_SPREAD

    def block_scores_factored(bs, r0, b, qa, kk):
        d = block_first_rows(r0, bs) - b
        zq = (qa * jnp.exp2(-d)).astype(BF16)
        zk = (kk * jnp.exp2(d)).astype(BF16)
        return _dot_nt(zq, zk)

    def block_scores_explicit(r0, b, qa, kk):
        cb_scr[...] = b - jnp.log2(kk)
        xs = []
        for s in range(c):
            pieces = []
            for blk in range(C // c):
                lo = blk * c
                d = b_scr[r0 + lo:r0 + lo + c, :] - cb_scr[lo + s:lo + s + 1, :]
                pieces.append(jnp.where(sub_row >= s, q_scr[r0 + lo:r0 + lo + c, :] * jnp.exp2(d), 0.0))
            xs.append(jnp.concatenate(pieces, axis=0).astype(BF16))
        return _dot(jnp.concatenate(xs, axis=1), sel_ref[...])

    def finish(block_scores, block_mask, first_level):
        for ci in range(chunks):
            r0 = ci * C
            rows = pl.ds(r0, C)
            b = b_scr[rows, :]
            kk = k_scr[rows, :]
            qa = q_scr[rows, :]
            v = i_ref[0, rows, :]
            st = st_scr[...]
            o = _dot_nt((qa * jnp.exp2(b)).astype(BF16), st.astype(BF16))

            p = mask_ref[block_mask] * block_scores(r0, b, qa, kk)
            for lvl, m in enumerate(HG_LEVELS):
                if m < first_level:
                    continue
                ref = jnp.concatenate(
                    [jnp.broadcast_to(b_scr[r0 + blk * 2 * m + m - 1:r0 + blk * 2 * m + m, :],
                                      (2 * m, HEAD_DIM)) for blk in range(C // (2 * m))], axis=0)
                upper = (row_id & m) != 0
                z = (jnp.where(upper, qa, kk)
                     * jnp.exp2((b - ref) * jnp.where(upper, 1.0, -1.0))).astype(BF16)
                p = p + mask_ref[lvl + 1] * _dot_nt(z, z)
            o = o + _dot(p.astype(BF16), v)

            b_last = b_scr[r0 + C - 1:r0 + C, :]
            kd = (kk * jnp.exp2(b_last - b)).astype(BF16)
            st_scr[...] = st * jnp.exp2(b_last) + _dot_tn(v, kd)

            gf = g_ref[0, rows, :].astype(F32)
            y = o * _rms_scale(o) * gn * (gf * _sigmoid(gf))
            o_ref[0, rows, :] = y.astype(o_ref.dtype)

    wide_mask = len(HG_LEVELS) + 1

    @pl.when(small_spread_wide)
    def _():
        finish(functools.partial(block_scores_factored, HG_WIDE), wide_mask, HG_WIDE)

    @pl.when(jnp.logical_not(small_spread_wide) & small_spread)
    def _():
        finish(functools.partial(block_scores_factored, c), 0, c)

    @pl.when(jnp.logical_not(small_spread))
    def _():
        finish(block_scores_explicit, 0, c)


def _hgrn(fraw, iqg, lb_logits, gnorm, *, iqg_col, layer, tt):
    b, s, width = fraw.shape
    nh = width // HEAD_DIM
    assert s % tt == 0 and tt % HG_CHUNK == 0 and iqg_col % HEAD_DIM == 0
    c0 = iqg_col // HEAD_DIM
    nl = lb_logits.shape[0]
    tri, sel, masks = _hgrn_constants()
    kern = functools.partial(_hgrn_kernel, layer=layer, chunks=tt // HG_CHUNK)
    blk = lambda off: pl.BlockSpec((1, tt, HEAD_DIM), lambda bi, h, t: (bi, t, off + h))
    const = lambda a: pl.BlockSpec(a.shape, lambda bi, h, t: (0,) * a.ndim)
    return pl.pallas_call(
        kern,
        grid=(b, nh, s // tt),
        in_specs=[
            blk(0),
            blk(c0),
            blk(c0 + nh),
            blk(c0 + 2 * nh),
            pl.BlockSpec((nl, HEAD_DIM), lambda bi, h, t: (0, h)),
            pl.BlockSpec((1, HEAD_DIM), lambda bi, h, t: (0, h)),
            const(tri), const(sel), const(masks),
        ],
        out_specs=blk(0),
        out_shape=jax.ShapeDtypeStruct((b, s, width), BF16),
        scratch_shapes=[
            pltpu.VMEM((HEAD_DIM, HEAD_DIM), F32),
            pltpu.VMEM((tt, HEAD_DIM), F32),
            pltpu.VMEM((tt, HEAD_DIM), F32),
            pltpu.VMEM((tt, HEAD_DIM), F32),
            pltpu.VMEM((HG_CHUNK, HEAD_DIM), F32),
        ],
        compiler_params=_params(("parallel", "parallel", "arbitrary")),
        name="hgrn",
    )(fraw, iqg, iqg, iqg, lb_logits, gnorm, tri, sel, masks)


def _merge_kernel(x_ref, ysb_ref, yhg_ref, gsb_ref, ghg_ref, wsb_ref, whg_ref, wout_ref, gn_ref,
                  o_ref, h_ref):
    a = _dot(ysb_ref[...], wsb_ref[...])
    b = _dot(yhg_ref[...], whg_ref[...])
    y = gsb_ref[...].astype(F32) * a + ghg_ref[...].astype(F32) * b
    x1 = x_ref[...] + _dot(y.astype(BF16), wout_ref[...])
    o_ref[...] = x1
    h_ref[...] = (x1 * _rms_scale(x1) * gn_ref[...]).astype(BF16)


def _const_spec(shape):
    return pl.BlockSpec(shape, lambda i: (0,) * len(shape), pipeline_mode=pl.Buffered(1))


def _merge(x2, ysb, yhg, gates, wsb, whg, wout, g_next, *, gates_col, tm):
    n, d = x2.shape
    wb = ysb.shape[1]
    assert gates_col % d == 0
    g0 = gates_col // d
    return pl.pallas_call(
        _merge_kernel,
        grid=(n // tm,),
        in_specs=[
            pl.BlockSpec((tm, d), lambda i: (i, 0)),
            pl.BlockSpec((tm, wb), lambda i: (i, 0)),
            pl.BlockSpec((tm, wb), lambda i: (i, 0)),
            pl.BlockSpec((tm, d), lambda i: (i, g0)),
            pl.BlockSpec((tm, d), lambda i: (i, g0 + 1)),
            _const_spec(wsb.shape),
            _const_spec(whg.shape),
            _const_spec(wout.shape),
            _const_spec(g_next.shape),
        ],
        out_specs=[pl.BlockSpec((tm, d), lambda i: (i, 0)), pl.BlockSpec((tm, d), lambda i: (i, 0))],
        out_shape=[jax.ShapeDtypeStruct((n, d), F32), jax.ShapeDtypeStruct((n, d), BF16)],
        compiler_params=_params(("parallel",)),
        name="merge",
    )(x2, ysb, yhg, gates, gates, wsb, whg, wout, g_next)


def _mlp_kernel(h_ref, wup_ref, wdn_ref, o_ref):
    def contribution():
        u = jnp.maximum(_dot(h_ref[...], wup_ref[...]), 0.0)
        return _dot((u * u).astype(BF16), wdn_ref[...])

    @pl.when(pl.program_id(1) == 0)
    def _():
        o_ref[...] = contribution()

    @pl.when(pl.program_id(1) > 0)
    def _():
        o_ref[...] += contribution()


def _mlp(h, wup, wdn, *, tm, tf):
    n, d = h.shape
    dff = wup.shape[1]
    return pl.pallas_call(
        _mlp_kernel,
        grid=(n // tm, dff // tf),
        in_specs=[
            pl.BlockSpec((tm, d), lambda i, f: (i, 0)),
            pl.BlockSpec((d, tf), lambda i, f: (0, f)),
            pl.BlockSpec((tf, d), lambda i, f: (f, 0)),
        ],
        out_specs=pl.BlockSpec((tm, d), lambda i, f: (i, 0)),
        out_shape=jax.ShapeDtypeStruct((n, d), F32),
        compiler_params=_params(("parallel", "arbitrary")),
        name="mlp",
    )(h, wup, wdn)


def _ple_kernel(x_ref, m_ref, p_ref, wg_ref, wp_ref, fg_ref, o_ref, *, final_norm):
    x = x_ref[...] + m_ref[...]
    gate = _sigmoid(_dot(x.astype(BF16), wg_ref[...]))
    y = x + gate * _dot(p_ref[...].astype(BF16), wp_ref[...])
    if final_norm:
        y = y * _rms_scale(y) * fg_ref[...]
    o_ref[...] = y


def _ple(x2, m2, p2, wg, wp, fg, *, tm, final_norm):
    n, d = x2.shape
    pd = p2.shape[1]
    kern = functools.partial(_ple_kernel, final_norm=final_norm)
    return pl.pallas_call(
        kern,
        grid=(n // tm,),
        in_specs=[
            pl.BlockSpec((tm, d), lambda i: (i, 0)),
            pl.BlockSpec((tm, d), lambda i: (i, 0)),
            pl.BlockSpec((tm, pd), lambda i: (i, 0)),
            _const_spec(wg.shape),
            _const_spec(wp.shape),
            _const_spec(fg.shape),
        ],
        out_specs=pl.BlockSpec((tm, d), lambda i: (i, 0)),
        out_shape=jax.ShapeDtypeStruct((n, d), F32),
        compiler_params=_params(("parallel",)),
        name="ple",
    )(x2, m2, p2, wg, wp, fg)


def _tile(n, pref):
    t = min(n, pref)
    assert n % t == 0
    return t


def kernel(x, p, mix_norm_g, w_in, hgrn_lb_logits, hgrn_out_norm_g, w_o_sb, w_o_hg, w_out,
           mlp_norm_g, w_up, w_down, w_ple_proj, w_ple_gate, final_norm_g):
    bsz, seq, d = x.shape
    depth = w_in.shape[0]
    n = bsz * seq
    sb_width = SB_HEADS * HEAD_DIM
    hg_width = HG_HEADS * HEAD_DIM
    x2 = x.reshape(n, d)
    for i in range(depth):
        assert sb_width == hg_width and d == 2 * hg_width
        tb = hg_width
        f_lo = 3 * sb_width
        w_bf = w_in[i].astype(BF16)
        act_width = w_bf.shape[1] - hg_width
        gates_col = act_width - 2 * d
        scale = jnp.ones((1, act_width), F32).at[:, :sb_width].set(-LOG2E * HEAD_DIM ** -0.5)
        g_mix = mix_norm_g[i].reshape(1, d)
        fraw, h_mix = _in_proj(x2, g_mix, w_bf, jnp.ones((1, hg_width), F32), [lambda j: f_lo // tb],
                               tm=_tile(n, 1024), tb=tb, steps=1, n_plain=1, out_dtype=F32,
                               name="in_proj_f")
        act = _in_proj(h_mix, None, w_bf, scale,
                       [lambda j: 2 * j + jnp.where(j >= 2, 1, 0),
                        lambda j: 2 * j + 1 + jnp.where(j >= 1, 1, 0)],
                       tm=_tile(n, 1024), tb=tb, steps=act_width // d,
                       n_plain=gates_col // d, out_dtype=BF16, name="in_proj")
        act3 = act.reshape(bsz, seq, act_width)
        ysb = _sb_attn(act3, width=sb_width, t=_tile(seq, 256))
        yhg = _hgrn(fraw.reshape(bsz, seq, hg_width), act3,
                    hgrn_lb_logits, hgrn_out_norm_g[i].reshape(1, hg_width),
                    iqg_col=f_lo, layer=i, tt=_tile(seq, 4096))
        x2, h_mlp = _merge(x2, ysb.reshape(n, sb_width), yhg.reshape(n, hg_width), act,
                           w_o_sb[i].astype(BF16), w_o_hg[i].astype(BF16), w_out[i].astype(BF16),
                           mlp_norm_g[i].reshape(1, d), gates_col=gates_col, tm=_tile(n, 256))
        m2 = _mlp(h_mlp, w_up[i].astype(BF16), w_down[i].astype(BF16), tm=_tile(n, 1024), tf=1024)
        x2 = _ple(x2, m2, p[i].reshape(n, p.shape[-1]), w_ple_gate[i].astype(BF16),
                  w_ple_proj[i].astype(BF16), final_norm_g.reshape(1, d),
                  tm=_tile(n, 512), final_norm=(i == depth - 1))
    return x2.reshape(bsz, seq, d)
```

```python
import functools

import numpy as np
import jax
import jax.numpy as jnp
from jax import lax
from jax.experimental import pallas as pl
from jax.experimental.pallas import tpu as pltpu

EPS = 1e-6
HEAD_DIM = 128
SB_HEADS = 8
HG_HEADS = 8
F32 = jnp.float32
BF16 = jnp.bfloat16
LOG2E = 1.4426950408889634

VMEM_LIMIT_BYTES = 56 * 1024 * 1024

EXP2_ZERO_BOUND = -151.0


def _params(sem):
    return pltpu.CompilerParams(dimension_semantics=sem, vmem_limit_bytes=VMEM_LIMIT_BYTES)


def _sigmoid(x):
    return 0.5 * jnp.tanh(0.5 * x) + 0.5


def _neg_abs(x):
    return -jnp.abs(x)


def _rms_scale(x):
    return lax.rsqrt(jnp.mean(x * x, axis=-1, keepdims=True) + EPS)


def _dot(a, b):
    return jnp.dot(a, b, preferred_element_type=F32)


def _dot_nt(a, b):
    return lax.dot_general(a, b, (((1,), (1,)), ((), ())), preferred_element_type=F32)


def _dot_tn(a, b):
    return lax.dot_general(a, b, (((0,), (0,)), ((), ())), preferred_element_type=F32)


def _in_proj_kernel(*refs, n_w, n_plain, normalize):
    if normalize:
        x_ref, g_ref, *rest = refs
        h_ref = rest[n_w + 2]
    else:
        x_ref, *rest = refs
        h_ref = x_ref
    w_refs, scale_ref, o_ref = rest[:n_w], rest[n_w], rest[n_w + 1]
    j = pl.program_id(1)
    tb = o_ref.shape[1] // n_w

    if normalize:
        @pl.when(j == 0)
        def _():
            x = x_ref[...]
            h_ref[...] = (x * _rms_scale(x) * g_ref[...]).astype(BF16)

    def store(finish):
        for b, w_ref in enumerate(w_refs):
            cols = slice(b * tb, (b + 1) * tb)
            o_ref[:, cols] = finish(_dot(h_ref[...], w_ref[...]), cols).astype(o_ref.dtype)

    @pl.when(j < n_plain)
    def _():
        store(lambda acc, cols: acc * scale_ref[:, cols])

    @pl.when(j >= n_plain)
    def _():
        store(lambda acc, cols: _sigmoid(acc))


def _in_proj(x2, g, w_bf16, scale, col_blocks, *, tm, tb, steps, n_plain, out_dtype, name):
    n, d = x2.shape
    assert n % tm == 0 and w_bf16.shape[1] % tb == 0
    normalize = g is not None
    n_w = len(col_blocks)
    tn = n_w * tb
    kern = functools.partial(_in_proj_kernel, n_w=n_w, n_plain=n_plain, normalize=normalize)
    row_spec = pl.BlockSpec((tm, d), lambda i, j: (i, 0))
    w_specs = [pl.BlockSpec((d, tb), lambda i, j, f=f: (0, f(j))) for f in col_blocks]
    proj_spec = pl.BlockSpec((tm, tn), lambda i, j: (i, j))
    proj_shape = jax.ShapeDtypeStruct((n, steps * tn), out_dtype)
    gain = [g] if normalize else []
    gain_spec = [pl.BlockSpec((1, d), lambda i, j: (0, 0))] if normalize else []
    return pl.pallas_call(
        kern,
        grid=(n // tm, steps),
        in_specs=[row_spec, *gain_spec, *w_specs, pl.BlockSpec((1, tn), lambda i, j: (0, j))],
        out_specs=[proj_spec, row_spec] if normalize else proj_spec,
        out_shape=[proj_shape, jax.ShapeDtypeStruct((n, d), BF16)] if normalize else proj_shape,
        compiler_params=_params(("parallel", "arbitrary")),
        name=name,
    )(x2, *gain, *([w_bf16] * n_w), scale)


def _sb_attn_kernel(q_ref, k_ref, v_ref, o_ref, acc_scr, r_scr, *, t, heads):
    i = pl.program_id(2)
    rows = heads * t

    jj = lax.broadcasted_iota(jnp.int32, (t, t), 0)
    ss = lax.broadcasted_iota(jnp.int32, (t, t), 1)
    later_mat = jnp.where(jj > ss, 1.0, 0.0).astype(BF16)

    def block(j, r_old):
        diagonal = r_old is None
        k_start = pl.multiple_of(j * t, t)
        zs = []
        for h in range(heads):
            lanes = slice(h * HEAD_DIM, (h + 1) * HEAD_DIM)
            zs.append(_dot_nt(q_ref[0, :, lanes], k_ref[0, pl.ds(k_start, t), lanes]))
        zn = jnp.concatenate(zs, axis=0)
        log_keep = jnp.minimum(zn, 0.0) - jnp.log2(1.0 + jnp.exp2(_neg_abs(zn)))
        if diagonal:
            row = lax.broadcasted_iota(jnp.int32, (rows, t), 0) & (t - 1)
            col = lax.broadcasted_iota(jnp.int32, (rows, t), 1)
            mask = col < row
            log_keep = jnp.where(mask, log_keep, 0.0)
        cs = _dot(log_keep.astype(BF16), later_mat)
        block_sum = cs[:, 0:1] + log_keep[:, 0:1]
        if diagonal:
            later, r_new = cs, block_sum
        else:
            later, r_new = cs + r_old, r_old + block_sum
        w = jnp.exp2((log_keep + later) - zn)
        if diagonal:
            w = jnp.where(mask, w, 0.0)
        w = w.astype(BF16)
        pv = [_dot(w[h * t:(h + 1) * t], v_ref[0, pl.ds(k_start, t), h * HEAD_DIM:(h + 1) * HEAD_DIM])
              for h in range(heads)]
        return pv, r_new

    def first_two():
        pv_d, r_d = block(i, None)
        pv_o, r_new = block(i - 1, r_d)
        for h in range(heads):
            acc_scr[h] = pv_d[h] + pv_o[h]
        r_scr[...] = r_new
        return jnp.max(r_new)

    def diagonal_only():
        pv_d, r_d = block(i, None)
        for h in range(heads):
            acc_scr[h] = pv_d[h]
        r_scr[...] = r_d
        return jnp.max(r_d)

    r_max0 = lax.cond(i > 0, first_two, diagonal_only)

    def body(carry):
        j, _ = carry
        pv, r_new = block(j, r_scr[...])
        for h in range(heads):
            acc_scr[h] += pv[h]
        r_scr[...] = r_new
        return j - 1, jnp.max(r_new)

    def cond(carry):
        j, r_max = carry
        return (j >= 0) & (r_max > EXP2_ZERO_BOUND)

    lax.while_loop(cond, body, (i - 2, r_max0))

    for h in range(heads):
        o_ref[0, :, h * HEAD_DIM:(h + 1) * HEAD_DIM] = acc_scr[h].astype(o_ref.dtype)


def _sb_attn(qkv, *, width, t):
    b, s, _ = qkv.shape
    heads = 4
    gw = heads * HEAD_DIM
    ng = width // gw
    assert s % t == 0 and t & (t - 1) == 0
    kern = functools.partial(_sb_attn_kernel, t=t, heads=heads)
    return pl.pallas_call(
        kern,
        grid=(b, ng, s // t),
        in_specs=[
            pl.BlockSpec((1, t, gw), lambda bi, g, i: (bi, i, g)),
            pl.BlockSpec((1, s, gw), lambda bi, g, i: (bi, 0, ng + g), pipeline_mode=pl.Buffered(1)),
            pl.BlockSpec((1, s, gw), lambda bi, g, i: (bi, 0, 2 * ng + g), pipeline_mode=pl.Buffered(1)),
        ],
        out_specs=pl.BlockSpec((1, t, gw), lambda bi, g, i: (bi, i, g)),
        out_shape=jax.ShapeDtypeStruct((b, s, width), BF16),
        scratch_shapes=[
            pltpu.VMEM((heads, t, HEAD_DIM), F32),
            pltpu.VMEM((heads * t, 1), F32),
        ],
        compiler_params=_params(("parallel", "parallel", "arbitrary")),
        name="sb_attn",
    )(qkv, qkv, qkv)


HG_CHUNK = 128
HG_SUB = 8
HG_LEVELS = (8, 16, 32, 64)
HG_WIDE = 32
HG_MAX_SPREAD = 100.0


def _hgrn_constants():
    C, c = HG_CHUNK, HG_SUB
    t = np.arange(C)[:, None]
    s = np.arange(C)[None, :]
    x = t ^ s
    tri = (s <= t).astype(np.float32)
    masks = [((x < c) & (s <= t)).astype(np.float32)]
    masks += [((t > s) & (x >= m) & (x < 2 * m)).astype(np.float32) for m in HG_LEVELS]
    masks += [((x < HG_WIDE) & (s <= t)).astype(np.float32)]
    r = np.arange(c * HEAD_DIM)[:, None]
    j = np.arange(C)[None, :]
    sel = ((j % c) == (r // HEAD_DIM)).astype(np.float32)
    return jnp.asarray(tri, BF16), jnp.asarray(sel, BF16), jnp.asarray(np.stack(masks), F32)


def _hgrn_kernel(fraw_ref, i_ref, q_ref, g_ref, lbl_ref, gn_ref, tri_ref, sel_ref, mask_ref, o_ref,
                 st_scr, b_scr, k_scr, q_scr, cb_scr, *, layer, chunks):
    C, c = HG_CHUNK, HG_SUB

    @pl.when(pl.program_id(2) == 0)
    def _():
        st_scr[...] = jnp.zeros_like(st_scr)

    logits = lbl_ref[...]
    e = jnp.exp(logits - jnp.max(logits, axis=0, keepdims=True))
    lb = jnp.sum(e[:layer + 1], axis=0, keepdims=True) / jnp.sum(e, axis=0, keepdims=True)
    gn = gn_ref[...]
    sub_row = lax.broadcasted_iota(jnp.int32, (c, HEAD_DIM), 0)
    row_id = lax.broadcasted_iota(jnp.int32, (C, HEAD_DIM), 0)

    def block_first_rows(r0, bs):
        return jnp.concatenate(
            [jnp.broadcast_to(b_scr[r0 + blk * bs:r0 + blk * bs + 1, :], (bs, HEAD_DIM))
             for blk in range(C // bs)], axis=0)

    spread = jnp.zeros((C, HEAD_DIM), F32)
    spread_wide = jnp.zeros((C, HEAD_DIM), F32)
    for ci in range(chunks):
        r0 = ci * C
        rows = pl.ds(r0, C)
        f = lb + (1.0 - lb) * _sigmoid(fraw_ref[0, rows, :])
        lf = jnp.log2(f)
        p0 = lf.astype(BF16)
        r1 = lf - p0.astype(F32)
        p1 = r1.astype(BF16)
        p2 = (r1 - p1.astype(F32)).astype(BF16)
        tri = tri_ref[...]
        b = _dot(tri, p0) + _dot(tri, p1) + _dot(tri, p2)
        qf = q_ref[0, rows, :].astype(F32)
        b_scr[rows, :] = b
        k_scr[rows, :] = 1.0 - f
        q_scr[rows, :] = qf * _sigmoid(qf)
        spread = jnp.maximum(spread, block_first_rows(r0, c) - b)
        spread_wide = jnp.maximum(spread_wide, block_first_rows(r0, HG_WIDE) - b)
    small_spread = jnp.max(spread) <= HG_MAX_SPREAD
    small_spread_wide = jnp.max(spread_wide) <= HG_MAX_SPREAD

    def block_scores_factored(bs, r0, b, qa, kk):
        d = block_first_rows(r0, bs) - b
        zq = (qa * jnp.exp2(-d)).astype(BF16)
        zk = (kk * jnp.exp2(d)).astype(BF16)
        return _dot_nt(zq, zk)

    def block_scores_explicit(r0, b, qa, kk):
        cb_scr[...] = b - jnp.log2(kk)
        xs = []
        for s in range(c):
            pieces = []
            for blk in range(C // c):
                lo = blk * c
                d = b_scr[r0 + lo:r0 + lo + c, :] - cb_scr[lo + s:lo + s + 1, :]
                pieces.append(jnp.where(sub_row >= s, q_scr[r0 + lo:r0 + lo + c, :] * jnp.exp2(d), 0.0))
            xs.append(jnp.concatenate(pieces, axis=0).astype(BF16))
        return _dot(jnp.concatenate(xs, axis=1), sel_ref[...])

    def finish(block_scores, block_mask, first_level):
        for ci in range(chunks):
            r0 = ci * C
            rows = pl.ds(r0, C)
            b = b_scr[rows, :]
            kk = k_scr[rows, :]
            qa = q_scr[rows, :]
            v = i_ref[0, rows, :]
            st = st_scr[...]
            o = _dot_nt((qa * jnp.exp2(b)).astype(BF16), st.astype(BF16))

            p = mask_ref[block_mask] * block_scores(r0, b, qa, kk)
            for lvl, m in enumerate(HG_LEVELS):
                if m < first_level:
                    continue
                ref = jnp.concatenate(
                    [jnp.broadcast_to(b_scr[r0 + blk * 2 * m + m - 1:r0 + blk * 2 * m + m, :],
                                      (2 * m, HEAD_DIM)) for blk in range(C // (2 * m))], axis=0)
                upper = (row_id & m) != 0
                z = (jnp.where(upper, qa, kk)
                     * jnp.exp2((b - ref) * jnp.where(upper, 1.0, -1.0))).astype(BF16)
                p = p + mask_ref[lvl + 1] * _dot_nt(z, z)
            o = o + _dot(p.astype(BF16), v)

            b_last = b_scr[r0 + C - 1:r0 + C, :]
            kd = (kk * jnp.exp2(b_last - b)).astype(BF16)
            st_scr[...] = st * jnp.exp2(b_last) + _dot_tn(v, kd)

            gf = g_ref[0, rows, :].astype(F32)
            y = o * _rms_scale(o) * gn * (gf * _sigmoid(gf))
            o_ref[0, rows, :] = y.astype(o_ref.dtype)

    wide_mask = len(HG_LEVELS) + 1

    @pl.when(small_spread_wide)
    def _():
        finish(functools.partial(block_scores_factored, HG_WIDE), wide_mask, HG_WIDE)

    @pl.when(jnp.logical_not(small_spread_wide) & small_spread)
    def _():
        finish(functools.partial(block_scores_factored, c), 0, c)

    @pl.when(jnp.logical_not(small_spread))
    def _():
        finish(block_scores_explicit, 0, c)


def _hgrn(fraw, iqg, lb_logits, gnorm, *, iqg_col, layer, tt):
    b, s, width = fraw.shape
    nh = width // HEAD_DIM
    assert s % tt == 0 and tt % HG_CHUNK == 0 and iqg_col % HEAD_DIM == 0
    c0 = iqg_col // HEAD_DIM
    nl = lb_logits.shape[0]
    tri, sel, masks = _hgrn_constants()
    kern = functools.partial(_hgrn_kernel, layer=layer, chunks=tt // HG_CHUNK)
    blk = lambda off: pl.BlockSpec((1, tt, HEAD_DIM), lambda bi, h, t: (bi, t, off + h))
    const = lambda a: pl.BlockSpec(a.shape, lambda bi, h, t: (0,) * a.ndim)
    return pl.pallas_call(
        kern,
        grid=(b, nh, s // tt),
        in_specs=[
            blk(0),
            blk(c0),
            blk(c0 + nh),
            blk(c0 + 2 * nh),
            pl.BlockSpec((nl, HEAD_DIM), lambda bi, h, t: (0, h)),
            pl.BlockSpec((1, HEAD_DIM), lambda bi, h, t: (0, h)),
            const(tri), const(sel), const(masks),
        ],
        out_specs=blk(0),
        out_shape=jax.ShapeDtypeStruct((b, s, width), BF16),
        scratch_shapes=[
            pltpu.VMEM((HEAD_DIM, HEAD_DIM), F32),
            pltpu.VMEM((tt, HEAD_DIM), F32),
            pltpu.VMEM((tt, HEAD_DIM), F32),
            pltpu.VMEM((tt, HEAD_DIM), F32),
            pltpu.VMEM((HG_CHUNK, HEAD_DIM), F32),
        ],
        compiler_params=_params(("parallel", "parallel", "arbitrary")),
        name="hgrn",
    )(fraw, iqg, iqg, iqg, lb_logits, gnorm, tri, sel, masks)


def _merge_kernel(x_ref, ysb_ref, yhg_ref, gsb_ref, ghg_ref, wsb_ref, whg_ref, wout_ref, gn_ref,
                  o_ref, h_ref):
    a = _dot(ysb_ref[...], wsb_ref[...])
    b = _dot(yhg_ref[...], whg_ref[...])
    y = gsb_ref[...].astype(F32) * a + ghg_ref[...].astype(F32) * b
    x1 = x_ref[...] + _dot(y.astype(BF16), wout_ref[...])
    o_ref[...] = x1
    h_ref[...] = (x1 * _rms_scale(x1) * gn_ref[...]).astype(BF16)


def _const_spec(shape):
    return pl.BlockSpec(shape, lambda i: (0,) * len(shape), pipeline_mode=pl.Buffered(1))


def _merge(x2, ysb, yhg, gates, wsb, whg, wout, g_next, *, gates_col, tm):
    n, d = x2.shape
    wb = ysb.shape[1]
    assert gates_col % d == 0
    g0 = gates_col // d
    return pl.pallas_call(
        _merge_kernel,
        grid=(n // tm,),
        in_specs=[
            pl.BlockSpec((tm, d), lambda i: (i, 0)),
            pl.BlockSpec((tm, wb), lambda i: (i, 0)),
            pl.BlockSpec((tm, wb), lambda i: (i, 0)),
            pl.BlockSpec((tm, d), lambda i: (i, g0)),
            pl.BlockSpec((tm, d), lambda i: (i, g0 + 1)),
            _const_spec(wsb.shape),
            _const_spec(whg.shape),
            _const_spec(wout.shape),
            _const_spec(g_next.shape),
        ],
        out_specs=[pl.BlockSpec((tm, d), lambda i: (i, 0)), pl.BlockSpec((tm, d), lambda i: (i, 0))],
        out_shape=[jax.ShapeDtypeStruct((n, d), F32), jax.ShapeDtypeStruct((n, d), BF16)],
        compiler_params=_params(("parallel",)),
        name="merge",
    )(x2, ysb, yhg, gates, gates, wsb, whg, wout, g_next)


def _mlp_kernel(h_ref, wup_ref, wdn_ref, o_ref):
    def contribution():
        u = jnp.maximum(_dot(h_ref[...], wup_ref[...]), 0.0)
        return _dot((u * u).astype(BF16), wdn_ref[...])

    @pl.when(pl.program_id(1) == 0)
    def _():
        o_ref[...] = contribution()

    @pl.when(pl.program_id(1) > 0)
    def _():
        o_ref[...] += contribution()


def _mlp(h, wup, wdn, *, tm, tf):
    n, d = h.shape
    dff = wup.shape[1]
    return pl.pallas_call(
        _mlp_kernel,
        grid=(n // tm, dff // tf),
        in_specs=[
            pl.BlockSpec((tm, d), lambda i, f: (i, 0)),
            pl.BlockSpec((d, tf), lambda i, f: (0, f)),
            pl.BlockSpec((tf, d), lambda i, f: (f, 0)),
        ],
        out_specs=pl.BlockSpec((tm, d), lambda i, f: (i, 0)),
        out_shape=jax.ShapeDtypeStruct((n, d), F32),
        compiler_params=_params(("parallel", "arbitrary")),
        name="mlp",
    )(h, wup, wdn)


def _ple_kernel(x_ref, m_ref, p_ref, wg_ref, wp_ref, fg_ref, o_ref, *, final_norm):
    x = x_ref[...] + m_ref[...]
    gate = _sigmoid(_dot(x.astype(BF16), wg_ref[...]))
    y = x + gate * _dot(p_ref[...].astype(BF16), wp_ref[...])
    if final_norm:
        y = y * _rms_scale(y) * fg_ref[...]
    o_ref[...] = y


def _ple(x2, m2, p2, wg, wp, fg, *, tm, final_norm):
    n, d = x2.shape
    pd = p2.shape[1]
    kern = functools.partial(_ple_kernel, final_norm=final_norm)
    return pl.pallas_call(
        kern,
        grid=(n // tm,),
        in_specs=[
            pl.BlockSpec((tm, d), lambda i: (i, 0)),
            pl.BlockSpec((tm, d), lambda i: (i, 0)),
            pl.BlockSpec((tm, pd), lambda i: (i, 0)),
            _const_spec(wg.shape),
            _const_spec(wp.shape),
            _const_spec(fg.shape),
        ],
        out_specs=pl.BlockSpec((tm, d), lambda i: (i, 0)),
        out_shape=jax.ShapeDtypeStruct((n, d), F32),
        compiler_params=_params(("parallel",)),
        name="ple",
    )(x2, m2, p2, wg, wp, fg)


def _tile(n, pref):
    t = min(n, pref)
    assert n % t == 0
    return t


def _tiles(n, seq):
    return dict(
        in_proj=_tile(n, 1024),
        attn=_tile(seq, 256),
        hgrn=_tile(seq, 4096),
        merge=_tile(n, 256),
        mlp=_tile(n, 1024),
        mlp_hidden=1024,
        ple=_tile(n, 512),
    )


def kernel(x, p, mix_norm_g, w_in, hgrn_lb_logits, hgrn_out_norm_g, w_o_sb, w_o_hg, w_out,
           mlp_norm_g, w_up, w_down, w_ple_proj, w_ple_gate, final_norm_g):
    bsz, seq, d = x.shape
    depth = w_in.shape[0]
    n = bsz * seq
    sb_width = SB_HEADS * HEAD_DIM
    hg_width = HG_HEADS * HEAD_DIM
    x2 = x.reshape(n, d)
    tiles = _tiles(n, seq)
    for i in range(depth):
        assert sb_width == hg_width and d == 2 * hg_width
        tb = hg_width
        f_lo = 3 * sb_width
        w_bf = w_in[i].astype(BF16)
        act_width = w_bf.shape[1] - hg_width
        gates_col = act_width - 2 * d
        scale = jnp.ones((1, act_width), F32).at[:, :sb_width].set(-LOG2E * HEAD_DIM ** -0.5)
        g_mix = mix_norm_g[i].reshape(1, d)
        fraw, h_mix = _in_proj(x2, g_mix, w_bf, jnp.ones((1, hg_width), F32), [lambda j: f_lo // tb],
                               tm=tiles["in_proj"], tb=tb, steps=1, n_plain=1, out_dtype=F32,
                               name="in_proj_f")
        act = _in_proj(h_mix, None, w_bf, scale,
                       [lambda j: 2 * j + jnp.where(j >= 2, 1, 0),
                        lambda j: 2 * j + 1 + jnp.where(j >= 1, 1, 0)],
                       tm=tiles["in_proj"], tb=tb, steps=act_width // d,
                       n_plain=gates_col // d, out_dtype=BF16, name="in_proj")
        act3 = act.reshape(bsz, seq, act_width)
        ysb = _sb_attn(act3, width=sb_width, t=tiles["attn"])
        yhg = _hgrn(fraw.reshape(bsz, seq, hg_width), act3,
                    hgrn_lb_logits, hgrn_out_norm_g[i].reshape(1, hg_width),
                    iqg_col=f_lo, layer=i, tt=tiles["hgrn"])
        x2, h_mlp = _merge(x2, ysb.reshape(n, sb_width), yhg.reshape(n, hg_width), act,
                           w_o_sb[i].astype(BF16), w_o_hg[i].astype(BF16), w_out[i].astype(BF16),
                           mlp_norm_g[i].reshape(1, d), gates_col=gates_col, tm=tiles["merge"])
        m2 = _mlp(h_mlp, w_up[i].astype(BF16), w_down[i].astype(BF16),
                  tm=tiles["mlp"], tf=tiles["mlp_hidden"])
        x2 = _ple(x2, m2, p[i].reshape(n, p.shape[-1]), w_ple_gate[i].astype(BF16),
                  w_ple_proj[i].astype(BF16), final_norm_g.reshape(1, d),
                  tm=tiles["ple"], final_norm=(i == depth - 1))
    return x2.reshape(bsz, seq, d)
```
